```python
import math
import jax, jax.numpy as jnp
from jax import lax
import numpy as np

D_MODEL = 1024
BATCH = 16
SEQ = 4096
DEPTH = 2

HEAD_DIM = 64
N_MIXERS = 4
MIX_W = D_MODEL // N_MIXERS
HEADS_PER_MIXER = MIX_W // HEAD_DIM
N_MIX_HEADS = N_MIXERS * HEADS_PER_MIXER
N_IN_PIECES = 11
IN_COLS = N_IN_PIECES * MIX_W
BLOCK = 128
SGU_CHUNK = 128
DILATED_PATTERNS = ((128, 1), (512, 4), (2048, 16))
CONV_WIDTH = 3
PEER_HEADS = 8
PEER_N_KEYS = 128
PEER_N_EXPERTS = PEER_N_KEYS * PEER_N_KEYS
PEER_HALF = 128
PEER_QUERY_DIM = 2 * PEER_HALF
PEER_TOPK = 16
PEER_TOKEN_CHUNK = 128
RMS_EPS = 1e-6
NEG = -1e30

kernel_name = "hybrid_sgu_dilated_conv_stickbreak_peer"


def _rms(x, g):
    xf = x.astype(jnp.float32)
    y = xf * lax.rsqrt(jnp.mean(xf * xf, axis=-1, keepdims=True) + RMS_EPS)
    return (y * g.astype(jnp.float32)).astype(x.dtype)


def _alibi_slopes(n):
    return jnp.asarray(np.array([2.0 ** (-8.0 * (i + 1) / n) for i in range(n)], dtype=np.float32))


def _sgu_mixer(u, v, w_s, b_s, g_v):
    B, S, _ = u.shape
    H, hd, C = HEADS_PER_MIXER, HEAD_DIM, SGU_CHUNK
    u = jax.nn.gelu(u).reshape(B, S, H, hd)
    v = _rms(jax.nn.gelu(v).reshape(B, S, H, hd), g_v.reshape(H, hd))
    v = v.reshape(B, S // C, C, H, hd)
    tril = jnp.tril(jnp.ones((C, C), dtype=bool))
    w = jnp.where(tril[None], w_s, 0.0).astype(v.dtype)
    mixed = jnp.einsum('hts,bcshd->bcthd', w, v) + b_s.T[None, None, :, :, None].astype(v.dtype)
    return (u * mixed.reshape(B, S, H, hd)).reshape(B, S, MIX_W)


def _dilated_pattern(q, k, v, window, dil, slopes):
    B, S, H, hd = q.shape
    L = S // dil
    steps = window // dil
    nb = -(-L // BLOCK)
    Lp = nb * BLOCK

    def to_sub(t):
        return t.reshape(B, L, dil, H, hd).transpose(0, 2, 3, 1, 4)

    pad_q = ((0, 0), (0, 0), (0, 0), (0, Lp - L), (0, 0))
    pad_kv = ((0, 0), (0, 0), (0, 0), (BLOCK, Lp - L), (0, 0))
    qb = jnp.pad(to_sub(q), pad_q).reshape(B, dil, H, nb, BLOCK, hd)
    kb = jnp.pad(to_sub(k), pad_kv).reshape(B, dil, H, nb + 1, BLOCK, hd)
    vb = jnp.pad(to_sub(v), pad_kv).reshape(B, dil, H, nb + 1, BLOCK, hd)
    kband = jnp.concatenate([kb[:, :, :, :-1], kb[:, :, :, 1:]], axis=4)
    vband = jnp.concatenate([vb[:, :, :, :-1], vb[:, :, :, 1:]], axis=4)
    s = jnp.einsum('brhnqd,brhnkd->brhnqk', qb, kband).astype(jnp.float32)
    qi = jnp.arange(BLOCK)
    kj = jnp.arange(2 * BLOCK)
    dist = qi[:, None] + BLOCK - kj[None, :]
    key_pos = jnp.arange(nb)[:, None] * BLOCK - BLOCK + kj[None, :]
    valid = ((dist >= 0) & (dist <= steps))[None] & (key_pos >= 0)[:, None, :]
    s = s - slopes[None, None, :, None, None, None] * (dist * dil).astype(jnp.float32)
    s = jnp.where(valid, s, NEG)
    lse = jax.nn.logsumexp(s, axis=-1)
    p = jnp.exp(s - lse[..., None]).astype(v.dtype)
    o = jnp.einsum('brhnqk,brhnkd->brhnqd', p, vband)
    o = o.reshape(B, dil, H, Lp, hd)[:, :, :, :L].transpose(0, 3, 1, 2, 4).reshape(B, S, H, hd)
    lse = lse.reshape(B, dil, H, Lp)[..., :L].transpose(0, 3, 1, 2).reshape(B, S, H)
    return o, lse


def _dilated_mixer(q, k, v):
    slopes = _alibi_slopes(q.shape[2])
    outs, lses = [], []
    for window, dil in DILATED_PATTERNS:
        o, l = _dilated_pattern(q, k, v, window, dil, slopes)
        outs.append(o)
        lses.append(l)
    w = jax.nn.softmax(jnp.stack(lses, axis=0), axis=0)
    o = jnp.sum(w[..., None].astype(q.dtype) * jnp.stack(outs, axis=0), axis=0)
    return o


def _short_conv_mixer(bg, cg, xt, w):
    y = cg * xt
    y = lax.conv_general_dilated(y, w[:, None, :].astype(y.dtype), window_strides=(1,),
                                 padding=((CONV_WIDTH - 1, 0),),
                                 dimension_numbers=('NWC', 'WIO', 'NWC'),
                                 feature_group_count=MIX_W)
    return bg * y


def _stick_breaking(q, k, v):
    B, S, H, hd = q.shape
    nb = S // BLOCK
    kt = k.transpose(0, 2, 1, 3)
    vt = v.transpose(0, 2, 1, 3)
    qb = q.transpose(0, 2, 1, 3).reshape(B, H, nb, BLOCK, hd).transpose(2, 0, 1, 3, 4)
    spos = jnp.arange(S)

    def block(args):
        qblk, i = args
        z = jnp.einsum('bhqd,bhkd->bhqk', qblk, kt).astype(jnp.float32)
        tpos = i * BLOCK + jnp.arange(BLOCK)
        causal = spos[None, :] < tpos[:, None]
        log_1m = jnp.where(causal, jax.nn.log_sigmoid(-z), 0.0)
        after = lax.cumsum(log_1m, axis=3, reverse=True) - log_1m
        a = jnp.where(causal, jnp.exp(jax.nn.log_sigmoid(z) + after), 0.0)
        return jnp.einsum('bhqk,bhkd->bhqd', a.astype(vt.dtype), vt)

    o = lax.map(block, (qb, jnp.arange(nb)))
    return o.transpose(1, 0, 3, 2, 4).reshape(B, S, H, hd)


def _mixer_block(h, w_in, w_sgu, b_sgu, g_sgu_v, g_q_dil, g_k_dil, conv_w,
                 g_q_sb, g_k_sb, g_head_out, w_out):
    B, S, _ = h.shape
    H, hd = HEADS_PER_MIXER, HEAD_DIM
    scale = HEAD_DIM ** -0.5
    proj = h @ w_in
    (su, sv, dq, dk, dv, cb, cc, cx, bq, bk, bv) = jnp.split(proj, N_IN_PIECES, axis=-1)
    heads = lambda t: t.reshape(B, S, H, hd)
    out_a = _sgu_mixer(su, sv, w_sgu, b_sgu, g_sgu_v)
    qd = _rms(heads(dq), g_q_dil) * scale
    kd = _rms(heads(dk), g_k_dil)
    out_b = _dilated_mixer(qd, kd, heads(dv)).reshape(B, S, MIX_W)
    out_c = _short_conv_mixer(cb, cc, cx, conv_w)
    qs = _rms(heads(bq), g_q_sb) * scale
    ks = _rms(heads(bk), g_k_sb)
    out_d = _stick_breaking(qs, ks, heads(bv)).reshape(B, S, MIX_W)
    cat = jnp.concatenate([out_a, out_b, out_c, out_d], axis=-1).reshape(B, S, N_MIX_HEADS, HEAD_DIM)
    cat = _rms(cat, g_head_out.reshape(N_MIX_HEADS, HEAD_DIM)).reshape(B, S, D_MODEL)
    return cat @ w_out


def _peer(h, w_q, sub_keys, u_tab, v_tab):
    B, S, D = h.shape
    C = PEER_TOKEN_CHUNK
    tok = h.reshape(-1, C, D)

    def chunk(xc):
        q = (xc @ w_q).reshape(C, PEER_HEADS, 2, PEER_HALF)
        s = jnp.einsum('thpd,hpkd->thpk', q, sub_keys).astype(jnp.float32)
        sv, si = lax.top_k(s, PEER_TOPK)
        cand = (sv[:, :, 0, :, None] + sv[:, :, 1, None, :]).reshape(C, PEER_HEADS, PEER_TOPK * PEER_TOPK)
        cidx = (si[:, :, 0, :, None] * PEER_N_KEYS + si[:, :, 1, None, :]).reshape(C, PEER_HEADS, PEER_TOPK * PEER_TOPK)
        top, pos = lax.top_k(cand, PEER_TOPK)
        eidx = jnp.take_along_axis(cidx, pos, axis=-1)
        g = jax.nn.softmax(top, axis=-1)
        a = jnp.einsum('thkd,td->thk', u_tab[eidx], xc).astype(jnp.float32)
        w = (g * jax.nn.gelu(a)).astype(xc.dtype)
        return jnp.einsum('thk,thkd->td', w, v_tab[eidx])

    return lax.map(chunk, tok).reshape(B, S, D)


def setup_inputs(seed: int = 0) -> dict:
    key = jax.random.key(seed)
    ks = jax.random.split(key, 20)
    f32 = jnp.float32
    nrm = lambda k, shape, s: jax.random.normal(k, shape, f32) * s
    gain = lambda k, shape: 1.0 + 0.02 * jax.random.normal(k, shape, f32)
    return {
        "x": jax.random.normal(ks[0], (BATCH, SEQ, D_MODEL), f32),
        "g_mix_norm": gain(ks[1], (DEPTH, D_MODEL)),
        "w_in": nrm(ks[2], (DEPTH, D_MODEL, IN_COLS), D_MODEL ** -0.5),
        "w_sgu": nrm(ks[3], (DEPTH, HEADS_PER_MIXER, SGU_CHUNK, SGU_CHUNK), SGU_CHUNK ** -0.5),
        "b_sgu": gain(ks[4], (DEPTH, HEADS_PER_MIXER, SGU_CHUNK)),
        "g_sgu_v": gain(ks[5], (DEPTH, MIX_W)),
        "g_q_dil": gain(ks[6], (DEPTH, HEAD_DIM)),
        "g_k_dil": gain(ks[7], (DEPTH, HEAD_DIM)),
        "conv_w": nrm(ks[8], (DEPTH, CONV_WIDTH, MIX_W), CONV_WIDTH ** -0.5),
        "g_q_sb": gain(ks[9], (DEPTH, HEAD_DIM)),
        "g_k_sb": gain(ks[10], (DEPTH, HEAD_DIM)),
        "g_head_out": gain(ks[11], (DEPTH, D_MODEL)),
        "w_out": nrm(ks[12], (DEPTH, D_MODEL, D_MODEL), D_MODEL ** -0.5),
        "g_ffn_norm": gain(ks[13], (DEPTH, D_MODEL)),
        "w_peer_q": nrm(ks[14], (DEPTH, D_MODEL, PEER_HEADS * PEER_QUERY_DIM), D_MODEL ** -0.5),
        "peer_sub_keys": nrm(ks[15], (DEPTH, PEER_HEADS, 2, PEER_N_KEYS, PEER_HALF), PEER_HALF ** -0.5),
        "peer_u": nrm(ks[16], (DEPTH, PEER_N_EXPERTS, D_MODEL), D_MODEL ** -0.5),
        "peer_v": nrm(ks[17], (DEPTH, PEER_N_EXPERTS, D_MODEL), PEER_HEADS ** -0.5),
    }


def reference(x, g_mix_norm, w_in, w_sgu, b_sgu, g_sgu_v, g_q_dil, g_k_dil, conv_w,
              g_q_sb, g_k_sb, g_head_out, w_out, g_ffn_norm, w_peer_q, peer_sub_keys,
              peer_u, peer_v):
    for l in range(DEPTH):
        h = _rms(x, g_mix_norm[l])
        x = x + _mixer_block(h, w_in[l], w_sgu[l], b_sgu[l], g_sgu_v[l], g_q_dil[l], g_k_dil[l],
                             conv_w[l], g_q_sb[l], g_k_sb[l], g_head_out[l], w_out[l])
        h = _rms(x, g_ffn_norm[l])
        x = x + _peer(h, w_peer_q[l], peer_sub_keys[l], peer_u[l], peer_v[l])
    return x
```

```python
import functools

import numpy as np
import jax
import jax.numpy as jnp
from jax import lax
from jax.experimental import pallas as pl
from jax.experimental.pallas import tpu as pltpu

HEAD_DIM = 64
MIX_W = 256
HEADS_PER_MIXER = MIX_W // HEAD_DIM
N_IN_PIECES = 11
SGU_CHUNK = 128
RMS_EPS = 1e-6
NEG = -1e30
PEER_HEADS = 8
PEER_N_KEYS = 128
PEER_TOPK = 16
ALIBI_SLOPES = tuple(2.0 ** (-8.0 * (i + 1) / HEADS_PER_MIXER) for i in range(HEADS_PER_MIXER))

LANES = 128
TM = 512
TQ = 256
DIL_MAX_DIST = 2048
DIL_NWIN = DIL_MAX_DIST // TQ + 1
DIL_WIN = DIL_NWIN * TQ
PEER_EC = 1024
VMEM_LIMIT = 56 * 1024 * 1024

F32 = jnp.float32
BF16 = jnp.bfloat16
_NT = (((1,), (1,)), ((), ()))


def _params(*sem):
    return pltpu.CompilerParams(dimension_semantics=sem, vmem_limit_bytes=VMEM_LIMIT)


def _head_sumsq(x2, bd):
    hi = x2.astype(BF16)
    lo = (x2 - hi.astype(F32)).astype(BF16)
    return (jnp.dot(hi, bd, preferred_element_type=F32) + jnp.dot(lo, bd, preferred_element_type=F32))


def _head_rms(x, g, bd):
    ss = _head_sumsq(x * x, bd)
    return x * lax.rsqrt(ss * (1.0 / HEAD_DIM) + RMS_EPS) * g


def _row_rms(x, g):
    ms = jnp.mean(x * x, axis=-1, keepdims=True)
    return x * lax.rsqrt(ms + RMS_EPS) * g


def _mixer_in_kernel(x_ref, gmix_ref, win_ref, wsgu_ref, bsgu_ref, gv_ref, gqd_ref, gkd_ref, convw_ref,
                     gqs_ref, gks_ref, bd_ref,
                     oa_ref, oc_ref, dq_ref, dk_ref, dv_ref, bq_ref, bk_ref, bv_ref,
                     ybuf_ref, *, tiles_per_seq):
    i = pl.program_id(0)
    tm = x_ref.shape[0]
    h = _row_rms(x_ref[...], gmix_ref[...]).astype(BF16)
    bd = bd_ref[...]
    scale = HEAD_DIM ** -0.5

    def piece(p):
        return jnp.dot(h, win_ref[:, p * MIX_W:(p + 1) * MIX_W], preferred_element_type=F32)

    u = jax.nn.gelu(piece(0))
    v = _head_rms(jax.nn.gelu(piece(1)), gv_ref[...], bd)
    t_idx = lax.broadcasted_iota(jnp.int32, (SGU_CHUNK, HEADS_PER_MIXER * SGU_CHUNK), 0)
    s_idx = lax.broadcasted_iota(jnp.int32, (SGU_CHUNK, HEADS_PER_MIXER * SGU_CHUNK), 1) & (SGU_CHUNK - 1)
    wcat = jnp.where(s_idx <= t_idx, wsgu_ref[...], 0.0).astype(BF16)
    lane_head = lax.broadcasted_iota(jnp.int32, (SGU_CHUNK, MIX_W), 1) >> 6
    bias = bsgu_ref[...]
    for c in range(tm // SGU_CHUNK):
        rows = slice(c * SGU_CHUNK, (c + 1) * SGU_CHUNK)
        vc = v[rows, :]
        vstack = jnp.concatenate(
            [jnp.where(lane_head == hh, vc, 0.0).astype(BF16) for hh in range(HEADS_PER_MIXER)], axis=0)
        mixed = jnp.dot(wcat, vstack, preferred_element_type=F32) + bias
        oa_ref[rows, :] = u[rows, :] * mixed

    dq_ref[...] = (_head_rms(piece(2), gqd_ref[...], bd) * scale).astype(BF16)
    dk_ref[...] = _head_rms(piece(3), gkd_ref[...], bd).astype(BF16)
    dv_ref[...] = piece(4).astype(BF16)
    bq_ref[...] = (_head_rms(piece(8), gqs_ref[...], bd) * scale).astype(BF16)
    bk_ref[...] = _head_rms(piece(9), gks_ref[...], bd).astype(BF16)
    bv_ref[...] = piece(10).astype(BF16)

    cb = piece(5)
    y = piece(6) * piece(7)

    @pl.when(i % tiles_per_seq == 0)
    def _():
        ybuf_ref[0:8, :] = jnp.zeros((8, MIX_W), F32)

    ybuf_ref[8:8 + tm, :] = y
    y1 = ybuf_ref[7:7 + tm, :]
    y2 = ybuf_ref[6:6 + tm, :]
    w = convw_ref[...]
    oc_ref[...] = cb * (w[0:1, :] * y2 + w[1:2, :] * y1 + w[2:3, :] * y)
    ybuf_ref[0:8, :] = y[tm - 8:tm, :]


def _mixer_in(x2d, gmix, win, wsgu_cat, bsgu_tile, gv, gqd, gkd, convw, gqs, gks, bd, *, seq):
    t = x2d.shape[0]
    d = x2d.shape[1]
    full = lambda a: pl.BlockSpec(a.shape, lambda i: (0,) * a.ndim)
    tok = lambda w: pl.BlockSpec((TM, w), lambda i: (i, 0))
    consts = (gmix, win, wsgu_cat, bsgu_tile, gv, gqd, gkd, convw, gqs, gks, bd)
    out_shape = ([jax.ShapeDtypeStruct((t, MIX_W), F32)] * 2 + [jax.ShapeDtypeStruct((t, MIX_W), BF16)] * 6)
    return pl.pallas_call(
        functools.partial(_mixer_in_kernel, tiles_per_seq=seq // TM),
        grid=(t // TM,),
        in_specs=[tok(d)] + [full(a) for a in consts],
        out_specs=[tok(MIX_W)] * 8,
        out_shape=out_shape,
        scratch_shapes=[pltpu.VMEM((TM + 8, MIX_W), F32)],
        compiler_params=_params("arbitrary"),
    )(x2d, *consts)


def _dilated_logcount_table():
    n = 2 * DIL_NWIN - 1
    qi = np.arange(TQ)[:, None]
    kj = np.arange(TQ)[None, :]
    tiles = []
    for u in range(n):
        off = DIL_NWIN - 1 - u
        dd = off * TQ + qi - kj
        cnt = ((dd >= 0) & (dd <= 128)).astype(np.int64)
        cnt += ((dd >= 0) & (dd <= 512) & (dd % 4 == 0))
        cnt += ((dd >= 0) & (dd <= 2048) & (dd % 16 == 0))
        tiles.append(np.where(cnt > 0, np.log(np.maximum(cnt, 1).astype(np.float64)), NEG))
    return np.concatenate(tiles, axis=1).astype(np.float32)


def _dilated_kernel(q_ref, k_ref, v_ref, lc_ref, o_ref):
    qb = pl.program_id(1)
    o0 = jnp.minimum(qb, DIL_NWIN - 1)
    kstart = pl.multiple_of((qb - o0) * TQ, TQ)
    cstart = pl.multiple_of((DIL_NWIN - 1 - o0) * TQ, TQ)
    lcw = lc_ref[:, pl.ds(cstart, DIL_WIN)]
    qi = lax.broadcasted_iota(jnp.int32, (TQ, DIL_WIN), 0)
    col = lax.broadcasted_iota(jnp.int32, (TQ, DIL_WIN), 1)
    dist = (o0 * TQ + qi - col).astype(F32)
    q = q_ref[0]
    kw = k_ref[0, pl.ds(kstart, DIL_WIN), :]
    vw = v_ref[0, pl.ds(kstart, DIL_WIN), :]
    lane_head = lax.broadcasted_iota(jnp.int32, (TQ, LANES), 1) >> 6
    for pair in range(2):
        cols = slice(pair * LANES, (pair + 1) * LANES)
        qp, kp, vp = q[:, cols], kw[:, cols], vw[:, cols]
        outs = []
        for hh in range(2):
            qm = jnp.where(lane_head == hh, qp, jnp.zeros_like(qp))
            s = lax.dot_general(qm, kp, _NT, preferred_element_type=F32)
            s = s + (lcw - ALIBI_SLOPES[2 * pair + hh] * dist)
            m = jnp.max(s, axis=-1, keepdims=True)
            p = jnp.exp(s - m)
            l = jnp.sum(p, axis=-1, keepdims=True)
            o = jnp.dot(p.astype(BF16), vp, preferred_element_type=F32)
            outs.append(o / l)
        o_ref[0, :, cols] = jnp.where(lane_head == 0, outs[0], outs[1])


def _dilated(q, k, v, lc):
    b, s, w = q.shape
    return pl.pallas_call(
        _dilated_kernel,
        grid=(b, s // TQ),
        in_specs=[pl.BlockSpec((1, TQ, w), lambda bi, qi: (bi, qi, 0)),
                  pl.BlockSpec((1, s, w), lambda bi, qi: (bi, 0, 0)),
                  pl.BlockSpec((1, s, w), lambda bi, qi: (bi, 0, 0)),
                  pl.BlockSpec(lc.shape, lambda bi, qi: (0, 0))],
        out_specs=pl.BlockSpec((1, TQ, w), lambda bi, qi: (bi, qi, 0)),
        out_shape=jax.ShapeDtypeStruct((b, s, w), F32),
        compiler_params=_params("arbitrary", "arbitrary"),
    )(q, k, v, lc)


def _suffix_sum_matrix():
    j = np.arange(TQ)[:, None]
    s = np.arange(TQ)[None, :]
    return np.concatenate([(j >= s), np.ones((TQ, LANES), bool)], axis=1).astype(np.float32)


def _stickbreak_kernel(q_ref, k_ref, v_ref, u_ref, o_ref, acc_ref, run_ref):
    qb = pl.program_id(1)
    q = q_ref[0]
    umat = u_ref[...]
    acc_ref[...] = jnp.zeros(acc_ref.shape, F32)
    run_ref[...] = jnp.zeros(run_ref.shape, F32)
    lane_head = lax.broadcasted_iota(jnp.int32, (TQ, LANES), 1) >> 6
    qm = []
    for pair in range(2):
        qp = q[:, pair * LANES:(pair + 1) * LANES]
        for hh in range(2):
            qm.append(jnp.where(lane_head == hh, qp, jnp.zeros_like(qp)))
    row = lax.broadcasted_iota(jnp.int32, (TQ, TQ), 0)
    colk = lax.broadcasted_iota(jnp.int32, (TQ, TQ), 1)
    causal = colk < row

    def tile(kb, diagonal):
        koff = pl.multiple_of(kb * TQ, TQ)
        kt = k_ref[0, pl.ds(koff, TQ), :]
        vt = v_ref[0, pl.ds(koff, TQ), :]
        for pair in range(2):
            cols = slice(pair * LANES, (pair + 1) * LANES)
            for hh in range(2):
                hd = 2 * pair + hh
                z = lax.dot_general(qm[hd], kt[:, cols], _NT, preferred_element_type=F32)
                lg = -(jnp.maximum(z, 0.0) + jnp.log1p(jnp.exp(-jnp.abs(z))))
                if diagonal:
                    lg = jnp.where(causal, lg, 0.0)
                hi = lg.astype(BF16)
                lo = (lg - hi.astype(F32)).astype(BF16)
                cs = (jnp.dot(hi, umat, preferred_element_type=F32)
                      + jnp.dot(lo, umat, preferred_element_type=F32))
                run = run_ref[hd]
                csum = cs[:, :TQ] + jnp.concatenate([run, run], axis=1)
                a = jnp.exp(z + csum)
                if diagonal:
                    a = jnp.where(causal, a, 0.0)
                pv = jnp.dot(a.astype(BF16), vt[:, cols], preferred_element_type=F32)
                acc_ref[:, cols] += jnp.where(lane_head == hh, pv, 0.0)
                run_ref[hd] = run + cs[:, TQ:]

    tile(qb, True)

    def body(j, carry):
        tile(qb - 1 - j, False)
        return carry

    lax.fori_loop(0, qb, body, 0)
    o_ref[0] = acc_ref[...]


def _stickbreak(q, k, v, umat):
    b, s, w = q.shape
    return pl.pallas_call(
        _stickbreak_kernel,
        grid=(b, s // TQ),
        in_specs=[pl.BlockSpec((1, TQ, w), lambda bi, qi: (bi, qi, 0)),
                  pl.BlockSpec((1, s, w), lambda bi, qi: (bi, 0, 0)),
                  pl.BlockSpec((1, s, w), lambda bi, qi: (bi, 0, 0)),
                  pl.BlockSpec(umat.shape, lambda bi, qi: (0, 0))],
        out_specs=pl.BlockSpec((1, TQ, w), lambda bi, qi: (bi, qi, 0)),
        out_shape=jax.ShapeDtypeStruct((b, s, w), F32),
        scratch_shapes=[pltpu.VMEM((TQ, w), F32), pltpu.VMEM((HEADS_PER_MIXER, TQ, LANES), F32)],
        compiler_params=_params("arbitrary", "arbitrary"),
    )(q, k, v, umat)


def _mixer_out_kernel(x_ref, oa_ref, ob_ref, oc_ref, od_ref, gho_ref, wout_ref, gffn_ref, bd_ref,
                      x1_ref, h2_ref):
    bd = bd_ref[...]
    acc = x_ref[...]
    for p, r in enumerate((oa_ref, ob_ref, oc_ref, od_ref)):
        cols = slice(p * MIX_W, (p + 1) * MIX_W)
        y = _head_rms(r[...], gho_ref[:, cols], bd).astype(BF16)
        acc = acc + jnp.dot(y, wout_ref[cols, :], preferred_element_type=F32)
    x1_ref[...] = acc
    h2_ref[...] = _row_rms(acc, gffn_ref[...]).astype(BF16)


def _mixer_out(x2d, oa, ob, oc, od, gho, wout, gffn, bd):
    t, d = x2d.shape
    full = lambda a: pl.BlockSpec(a.shape, lambda i: (0,) * a.ndim)
    tok = lambda w: pl.BlockSpec((TM, w), lambda i: (i, 0))
    return pl.pallas_call(
        _mixer_out_kernel,
        grid=(t // TM,),
        in_specs=[tok(d)] + [tok(MIX_W)] * 4 + [full(gho), full(wout), full(gffn), full(bd)],
        out_specs=[tok(d), tok(d)],
        out_shape=[jax.ShapeDtypeStruct((t, d), F32), jax.ShapeDtypeStruct((t, d), BF16)],
        compiler_params=_params("arbitrary"),
    )(x2d, oa, ob, oc, od, gho, wout, gffn, bd)


def _top16_rows(s):
    kidx = lax.broadcasted_iota(jnp.int32, s.shape, 0).astype(F32)
    rank = jnp.full(s.shape, float(PEER_TOPK), F32)
    vals = []
    for r in range(PEER_TOPK):
        m = jnp.max(s, axis=0, keepdims=True)
        first = jnp.min(jnp.where(s == m, kidx, float(PEER_N_KEYS)), axis=0, keepdims=True)
        hit = kidx == first
        rank = jnp.where(hit, float(r), rank)
        s = jnp.where(hit, -jnp.inf, s)
        vals.append(m)
    return vals, rank


def _peer_route_kernel(h2_ref, wqt_ref, keys_ref, r2_ref, e2_ref, n_ref, e1_ref, st_ref):
    tm = h2_ref.shape[0]
    h2 = h2_ref[...]
    for hp in range(2 * PEER_HEADS):
        rows = slice(hp * PEER_N_KEYS, (hp + 1) * PEER_N_KEYS)
        qt = lax.dot_general(wqt_ref[rows, :], h2, _NT, preferred_element_type=F32)
        st_ref[hp] = jnp.dot(keys_ref[hp], qt.astype(BF16), preferred_element_type=F32)

    nlb = tm // LANES
    sub = lax.broadcasted_iota(jnp.int32, (8, LANES), 0).astype(F32)
    pos = jnp.concatenate([sub, sub + 8.0] + [sub + 16.0 * a for a in range(1, 8)] + [(sub + 8.0) * 16.0], axis=0)

    def body(it, carry):
        hd = it // nlb
        lo = pl.multiple_of((it % nlb) * LANES, LANES)
        s1 = st_ref[2 * hd, :, pl.ds(lo, LANES)]
        s2 = st_ref[2 * hd + 1, :, pl.ds(lo, LANES)]
        v1, rk1 = _top16_rows(s1)
        v2, rk2 = _top16_rows(s2)
        sub16 = lax.broadcasted_iota(jnp.int32, (PEER_TOPK, LANES), 0).astype(F32)
        sv1 = jnp.zeros((PEER_TOPK, LANES), F32)
        sv2 = jnp.zeros((PEER_TOPK, LANES), F32)
        for r in range(PEER_TOPK):
            sv1 = jnp.where(sub16 == float(r), v1[r], sv1)
            sv2 = jnp.where(sub16 == float(r), v2[r], sv2)
        cand = jnp.concatenate(
            [v1[0] + sv2[0:8], v1[0] + sv2[8:16]] + [v1[a] + sv2[0:8] for a in range(1, 8)] + [sv1[8:16] + v2[0]],
            axis=0)
        cnt = jnp.zeros((PEER_N_KEYS, LANES), F32)
        zsum = jnp.zeros((1, LANES), F32)
        top0 = v1[0] + v2[0]
        for r in range(PEER_TOPK):
            m = jnp.max(cand, axis=0, keepdims=True)
            pr = jnp.min(jnp.where(cand == m, pos, 4096.0), axis=0, keepdims=True)
            cand = jnp.where(pos == pr, -jnp.inf, cand)
            cnt = cnt + jnp.where(rk1 == jnp.floor(pr * (1.0 / 16.0)), 1.0, 0.0)
            zsum = zsum + jnp.exp(m - top0)
        inv = 1.0 / zsum
        r2_ref[hd, :, pl.ds(lo, LANES)] = rk2
        e2_ref[hd, :, pl.ds(lo, LANES)] = jnp.exp(s2 - v2[0])
        n_ref[hd, :, pl.ds(lo, LANES)] = cnt
        e1_ref[hd, :, pl.ds(lo, LANES)] = jnp.exp(s1 - v1[0]) * inv
        return carry

    lax.fori_loop(0, PEER_HEADS * nlb, body, 0)


def _peer_route(h2, wqt, keys):
    t, d = h2.shape
    tabs = jax.ShapeDtypeStruct((PEER_HEADS, PEER_N_KEYS, t), F32)
    tab_spec = pl.BlockSpec((PEER_HEADS, PEER_N_KEYS, TM), lambda i: (0, 0, i))
    return pl.pallas_call(
        _peer_route_kernel,
        grid=(t // TM,),
        in_specs=[pl.BlockSpec((TM, d), lambda i: (i, 0)),
                  pl.BlockSpec(wqt.shape, lambda i: (0, 0)),
                  pl.BlockSpec(keys.shape, lambda i: (0, 0, 0))],
        out_specs=[tab_spec] * 4,
        out_shape=[tabs] * 4,
        scratch_shapes=[pltpu.VMEM((2 * PEER_HEADS, PEER_N_KEYS, TM), F32)],
        compiler_params=_params("arbitrary"),
    )(h2, wqt, keys)


def _peer_ffn_kernel(h2_ref, u_ref, vt_ref, r2_ref, e2_ref, n_ref, e1_ref, x1_ref, o_ref, acc_ref, wt_ref):
    c = pl.program_id(1)
    tm = h2_ref.shape[0]
    ec = u_ref.shape[0]

    @pl.when(c == 0)
    def _():
        acc_ref[...] = jnp.zeros(acc_ref.shape, F32)

    h2 = h2_ref[...]
    at = lax.dot_general(u_ref[...], h2, _NT, preferred_element_type=F32)
    for ii in range(ec // PEER_N_KEYS):
        i = c * (ec // PEER_N_KEYS) + ii
        rows = slice(ii * PEER_N_KEYS, (ii + 1) * PEER_N_KEYS)
        g = jnp.zeros((PEER_N_KEYS, tm), F32)
        for hd in range(PEER_HEADS):
            nrow = n_ref[hd, pl.ds(i, 1), :]
            e1row = e1_ref[hd, pl.ds(i, 1), :]
            g = g + jnp.where(r2_ref[hd] < nrow, e1row * e2_ref[hd], 0.0)
        wt_ref[rows, :] = (jax.nn.gelu(at[rows, :]) * g).astype(BF16)
    acc_ref[...] += jnp.dot(vt_ref[...], wt_ref[...], preferred_element_type=F32)

    @pl.when(c == pl.num_programs(1) - 1)
    def _():
        o_ref[...] = x1_ref[...] + acc_ref[...].T


def _peer_ffn(h2, u, vt, r2, e2, n, e1, x1):
    t, d = h2.shape
    ne = u.shape[0]
    tab_spec = pl.BlockSpec((PEER_HEADS, PEER_N_KEYS, TM), lambda i, c: (0, 0, i))
    return pl.pallas_call(
        _peer_ffn_kernel,
        grid=(t // TM, ne // PEER_EC),
        in_specs=[pl.BlockSpec((TM, d), lambda i, c: (i, 0)),
                  pl.BlockSpec((PEER_EC, d), lambda i, c: (c, 0)),
                  pl.BlockSpec((d, PEER_EC), lambda i, c: (0, c)),
                  tab_spec, tab_spec, tab_spec, tab_spec,
                  pl.BlockSpec((TM, d), lambda i, c: (i, 0))],
        out_specs=pl.BlockSpec((TM, d), lambda i, c: (i, 0)),
        out_shape=jax.ShapeDtypeStruct((t, d), F32),
        scratch_shapes=[pltpu.VMEM((d, TM), F32), pltpu.VMEM((PEER_EC, TM), BF16)],
        compiler_params=_params("arbitrary", "arbitrary"),
    )(h2, u, vt, r2, e2, n, e1, x1)


def _tile_heads(g):
    return jnp.tile(g, HEADS_PER_MIXER)[None, :]


def kernel(x, g_mix_norm, w_in, w_sgu, b_sgu, g_sgu_v, g_q_dil, g_k_dil, conv_w, g_q_sb, g_k_sb, g_head_out,
           w_out, g_ffn_norm, w_peer_q, peer_sub_keys, peer_u, peer_v):
    b, s, d = x.shape
    depth = w_in.shape[0]
    assert d == 4 * MIX_W and s % TM == 0 and s >= DIL_WIN
    assert w_in.shape[2] == N_IN_PIECES * MIX_W

    hh = np.arange(MIX_W) // HEAD_DIM
    bd = jnp.asarray(hh[:, None] == hh[None, :], BF16)
    lc = jnp.asarray(_dilated_logcount_table())
    umat = jnp.asarray(_suffix_sum_matrix(), BF16)

    x2d = x.reshape(b * s, d)
    for l in range(depth):
        wsgu_cat = w_sgu[l].transpose(1, 0, 2).reshape(SGU_CHUNK, HEADS_PER_MIXER * SGU_CHUNK)
        bsgu_tile = jnp.repeat(b_sgu[l].T, HEAD_DIM, axis=1)
        oa, oc, dq, dk, dv, bq, bk, bv = _mixer_in(
            x2d, g_mix_norm[l][None, :], w_in[l].astype(BF16), wsgu_cat, bsgu_tile, g_sgu_v[l][None, :],
            _tile_heads(g_q_dil[l]), _tile_heads(g_k_dil[l]), conv_w[l], _tile_heads(g_q_sb[l]),
            _tile_heads(g_k_sb[l]), bd, seq=s)
        r3 = lambda a: a.reshape(b, s, MIX_W)
        ob = _dilated(r3(dq), r3(dk), r3(dv), lc).reshape(b * s, MIX_W)
        od = _stickbreak(r3(bq), r3(bk), r3(bv), umat).reshape(b * s, MIX_W)
        x1, h2 = _mixer_out(x2d, oa, ob, oc, od, g_head_out[l][None, :], w_out[l].astype(BF16),
                            g_ffn_norm[l][None, :], bd)
        wqt = w_peer_q[l].T.astype(BF16)
        keys = peer_sub_keys[l].reshape(2 * PEER_HEADS, PEER_N_KEYS, -1).astype(BF16)
        r2, e2, n, e1 = _peer_route(h2, wqt, keys)
        x2d = _peer_ffn(h2, peer_u[l].astype(BF16), peer_v[l].T.astype(BF16), r2, e2, n, e1, x1)
    return x2d.reshape(b, s, d)
```

```python
import functools

import numpy as np
import jax
import jax.numpy as jnp
from jax import lax
from jax.experimental import pallas as pl
from jax.experimental.pallas import tpu as pltpu

HEAD_DIM = 64
MIX_W = 256
HEADS_PER_MIXER = MIX_W // HEAD_DIM
N_IN_PIECES = 11
SGU_CHUNK = 128
RMS_EPS = 1e-6
NEG = -1e30
LOG2E = 1.4426950408889634
PEER_HEADS = 8
PEER_N_KEYS = 128
PEER_TOPK = 16
ALIBI_SLOPES = tuple(2.0 ** (-8.0 * (i + 1) / HEADS_PER_MIXER) for i in range(HEADS_PER_MIXER))

LANES = 128
TM = 512
TQ = 256
DIL_MAX_DIST = 2048
DIL_NWIN = DIL_MAX_DIST // TQ + 1
DIL_WIN = DIL_NWIN * TQ
PEER_EC = 1024
VMEM_LIMIT = 56 * 1024 * 1024

F32 = jnp.float32
BF16 = jnp.bfloat16
_NT = (((1,), (1,)), ((), ()))


def _params(*sem, flags=None):
    return pltpu.CompilerParams(dimension_semantics=sem, vmem_limit_bytes=VMEM_LIMIT, flags=flags)


def _head_sumsq(x2, bd):
    hi = x2.astype(BF16)
    lo = (x2 - hi.astype(F32)).astype(BF16)
    return (jnp.dot(hi, bd, preferred_element_type=F32) + jnp.dot(lo, bd, preferred_element_type=F32))


def _head_rms(x, g, bd):
    ss = _head_sumsq(x * x, bd)
    return x * lax.rsqrt(ss * (1.0 / HEAD_DIM) + RMS_EPS) * g


def _row_rms(x, g):
    ms = jnp.mean(x * x, axis=-1, keepdims=True)
    return x * lax.rsqrt(ms + RMS_EPS) * g


def _mixer_in_kernel(x_ref, gmix_ref, win_ref, wsgu_ref, bsgu_ref, gv_ref, gqd_ref, gkd_ref, convw_ref,
                     gqs_ref, gks_ref, bd_ref,
                     oa_ref, oc_ref, dq_ref, dk_ref, dv_ref, bq_ref, bk_ref, bv_ref,
                     ybuf_ref, *, tiles_per_seq):
    i = pl.program_id(0)
    tm = x_ref.shape[0]
    h = _row_rms(x_ref[...], gmix_ref[...]).astype(BF16)
    bd = bd_ref[...]
    scale = HEAD_DIM ** -0.5

    def piece(p):
        return jnp.dot(h, win_ref[:, p * MIX_W:(p + 1) * MIX_W], preferred_element_type=F32)

    u = jax.nn.gelu(piece(0))
    v = _head_rms(jax.nn.gelu(piece(1)), gv_ref[...], bd)
    t_idx = lax.broadcasted_iota(jnp.int32, (SGU_CHUNK, HEADS_PER_MIXER * SGU_CHUNK), 0)
    s_idx = lax.broadcasted_iota(jnp.int32, (SGU_CHUNK, HEADS_PER_MIXER * SGU_CHUNK), 1) & (SGU_CHUNK - 1)
    wcat = jnp.where(s_idx <= t_idx, wsgu_ref[...], 0.0).astype(BF16)
    lane_head = lax.broadcasted_iota(jnp.int32, (SGU_CHUNK, MIX_W), 1) >> 6
    bias = bsgu_ref[...]
    for c in range(tm // SGU_CHUNK):
        rows = slice(c * SGU_CHUNK, (c + 1) * SGU_CHUNK)
        vc = v[rows, :]
        vstack = jnp.concatenate(
            [jnp.where(lane_head == hh, vc, 0.0).astype(BF16) for hh in range(HEADS_PER_MIXER)], axis=0)
        mixed = jnp.dot(wcat, vstack, preferred_element_type=F32) + bias
        oa_ref[rows, :] = u[rows, :] * mixed

    dq_ref[...] = (_head_rms(piece(2), gqd_ref[...], bd) * scale).astype(BF16)
    dk_ref[...] = _head_rms(piece(3), gkd_ref[...], bd).astype(BF16)
    dv_ref[...] = piece(4).astype(BF16)
    bq_ref[...] = (_head_rms(piece(8), gqs_ref[...], bd) * scale).astype(BF16)
    bk_ref[...] = _head_rms(piece(9), gks_ref[...], bd).astype(BF16)
    bv_ref[...] = piece(10).astype(BF16)

    cb = piece(5)
    y = piece(6) * piece(7)

    @pl.when(i % tiles_per_seq == 0)
    def _():
        ybuf_ref[0:8, :] = jnp.zeros((8, MIX_W), F32)

    ybuf_ref[8:8 + tm, :] = y
    y1 = ybuf_ref[7:7 + tm, :]
    y2 = ybuf_ref[6:6 + tm, :]
    w = convw_ref[...]
    oc_ref[...] = cb * (w[0:1, :] * y2 + w[1:2, :] * y1 + w[2:3, :] * y)
    ybuf_ref[0:8, :] = y[tm - 8:tm, :]


def _mixer_in(x2d, gmix, win, wsgu_cat, bsgu_tile, gv, gqd, gkd, convw, gqs, gks, bd, *, seq):
    t = x2d.shape[0]
    d = x2d.shape[1]
    full = lambda a: pl.BlockSpec(a.shape, lambda i: (0,) * a.ndim)
    tok = lambda w: pl.BlockSpec((TM, w), lambda i: (i, 0))
    consts = (gmix, win, wsgu_cat, bsgu_tile, gv, gqd, gkd, convw, gqs, gks, bd)
    out_shape = ([jax.ShapeDtypeStruct((t, MIX_W), F32)] * 2 + [jax.ShapeDtypeStruct((t, MIX_W), BF16)] * 6)
    return pl.pallas_call(
        functools.partial(_mixer_in_kernel, tiles_per_seq=seq // TM),
        grid=(t // TM,),
        in_specs=[tok(d)] + [full(a) for a in consts],
        out_specs=[tok(MIX_W)] * 8,
        out_shape=out_shape,
        scratch_shapes=[pltpu.VMEM((TM + 8, MIX_W), F32)],
        compiler_params=_params("arbitrary"),
    )(x2d, *consts)


def _dilated_logcount_table():
    n = 2 * DIL_NWIN - 1
    qi = np.arange(TQ)[:, None]
    kj = np.arange(TQ)[None, :]
    tiles = []
    for u in range(n):
        off = DIL_NWIN - 1 - u
        dd = off * TQ + qi - kj
        cnt = ((dd >= 0) & (dd <= 128)).astype(np.int64)
        cnt += ((dd >= 0) & (dd <= 512) & (dd % 4 == 0))
        cnt += ((dd >= 0) & (dd <= 2048) & (dd % 16 == 0))
        tiles.append(np.where(cnt > 0, np.log(np.maximum(cnt, 1).astype(np.float64)), NEG))
    return np.concatenate(tiles, axis=1).astype(np.float32)


def _dilated_kernel(q_ref, k_ref, v_ref, lc_ref, o_ref):
    qb = pl.program_id(1)
    o0 = jnp.minimum(qb, DIL_NWIN - 1)
    kstart = pl.multiple_of((qb - o0) * TQ, TQ)
    cstart = pl.multiple_of((DIL_NWIN - 1 - o0) * TQ, TQ)
    lcw = lc_ref[:, pl.ds(cstart, DIL_WIN)]
    qi = lax.broadcasted_iota(jnp.int32, (TQ, DIL_WIN), 0)
    col = lax.broadcasted_iota(jnp.int32, (TQ, DIL_WIN), 1)
    dist = (o0 * TQ + qi - col).astype(F32)
    q = q_ref[0]
    kw = k_ref[0, pl.ds(kstart, DIL_WIN), :]
    vw = v_ref[0, pl.ds(kstart, DIL_WIN), :]
    lane_head = lax.broadcasted_iota(jnp.int32, (TQ, LANES), 1) >> 6
    for pair in range(2):
        cols = slice(pair * LANES, (pair + 1) * LANES)
        qp, kp, vp = q[:, cols], kw[:, cols], vw[:, cols]
        outs = []
        for hh in range(2):
            qm = jnp.where(lane_head == hh, qp, jnp.zeros_like(qp))
            s = lax.dot_general(qm, kp, _NT, preferred_element_type=F32)
            s = s + (lcw - ALIBI_SLOPES[2 * pair + hh] * dist)
            m = jnp.max(s, axis=-1, keepdims=True)
            p = jnp.exp(s - m)
            l = jnp.sum(p, axis=-1, keepdims=True)
            o = jnp.dot(p.astype(BF16), vp, preferred_element_type=F32)
            outs.append(o / l)
        o_ref[0, :, cols] = jnp.where(lane_head == 0, outs[0], outs[1])


def _dilated(q, k, v, lc):
    b, s, w = q.shape
    return pl.pallas_call(
        _dilated_kernel,
        grid=(b, s // TQ),
        in_specs=[pl.BlockSpec((1, TQ, w), lambda bi, qi: (bi, qi, 0)),
                  pl.BlockSpec((1, s, w), lambda bi, qi: (bi, 0, 0)),
                  pl.BlockSpec((1, s, w), lambda bi, qi: (bi, 0, 0)),
                  pl.BlockSpec(lc.shape, lambda bi, qi: (0, 0))],
        out_specs=pl.BlockSpec((1, TQ, w), lambda bi, qi: (bi, qi, 0)),
        out_shape=jax.ShapeDtypeStruct((b, s, w), F32),
        compiler_params=_params("arbitrary", "arbitrary"),
    )(q, k, v, lc)


def _suffix_sum_matrix():
    j = np.arange(TQ)[:, None]
    s = np.arange(TQ)[None, :]
    return np.concatenate([(j >= s), np.ones((TQ, LANES), bool)], axis=1).astype(np.float32)


def _stickbreak_kernel(q_ref, k_ref, v_ref, u_ref, o_ref, acc_ref, run_ref):
    qb = pl.program_id(1)
    q = q_ref[0]
    umat = u_ref[...]
    acc_ref[...] = jnp.zeros(acc_ref.shape, F32)
    run_ref[...] = jnp.zeros(run_ref.shape, F32)
    lane_head = lax.broadcasted_iota(jnp.int32, (TQ, LANES), 1) >> 6
    qm = []
    for pair in range(2):
        qp = q[:, pair * LANES:(pair + 1) * LANES]
        for hh in range(2):
            qm.append(jnp.where(lane_head == hh, qp, jnp.zeros_like(qp)))
    row = lax.broadcasted_iota(jnp.int32, (TQ, TQ), 0)
    colk = lax.broadcasted_iota(jnp.int32, (TQ, TQ), 1)
    causal = colk < row

    def tile(kb, diagonal):
        koff = pl.multiple_of(kb * TQ, TQ)
        kt = k_ref[0, pl.ds(koff, TQ), :]
        vt = v_ref[0, pl.ds(koff, TQ), :]
        for pair in range(2):
            cols = slice(pair * LANES, (pair + 1) * LANES)
            for hh in range(2):
                hd = 2 * pair + hh
                y = lax.dot_general(qm[hd], kt[:, cols], _NT, preferred_element_type=F32) * LOG2E
                nl = jnp.maximum(y, 0.0) + jnp.log2(1.0 + jnp.exp2(-jnp.abs(y)))
                if diagonal:
                    nl = jnp.where(causal, nl, 0.0)
                cs = jnp.dot(nl.astype(BF16), umat, preferred_element_type=F32)
                run = run_ref[hd]
                csum = cs[:, :TQ] + jnp.concatenate([run, run], axis=1)
                a = jnp.exp2(y - csum)
                if diagonal:
                    a = jnp.where(causal, a, 0.0)
                pv = jnp.dot(a.astype(BF16), vt[:, cols], preferred_element_type=F32)
                acc_ref[:, cols] += jnp.where(lane_head == hh, pv, 0.0)
                run_ref[hd] = run + cs[:, TQ:]

    tile(qb, True)

    def body(j, carry):
        tile(qb - 1 - j, False)
        return carry

    lax.fori_loop(0, qb, body, 0)
    o_ref[0] = acc_ref[...]


def _stickbreak(q, k, v, umat):
    b, s, w = q.shape
    return pl.pallas_call(
        _stickbreak_kernel,
        grid=(b, s // TQ),
        in_specs=[pl.BlockSpec((1, TQ, w), lambda bi, qi: (bi, qi, 0)),
                  pl.BlockSpec((1, s, w), lambda bi, qi: (bi, 0, 0)),
                  pl.BlockSpec((1, s, w), lambda bi, qi: (bi, 0, 0)),
                  pl.BlockSpec(umat.shape, lambda bi, qi: (0, 0))],
        out_specs=pl.BlockSpec((1, TQ, w), lambda bi, qi: (bi, qi, 0)),
        out_shape=jax.ShapeDtypeStruct((b, s, w), F32),
        scratch_shapes=[pltpu.VMEM((TQ, w), F32), pltpu.VMEM((HEADS_PER_MIXER, TQ, LANES), F32)],
        compiler_params=_params("arbitrary", "arbitrary"),
    )(q, k, v, umat)


def _mixer_out_kernel(x_ref, oa_ref, ob_ref, oc_ref, od_ref, gho_ref, wout_ref, gffn_ref, bd_ref,
                      x1_ref, h2_ref):
    bd = bd_ref[...]
    acc = x_ref[...]
    for p, r in enumerate((oa_ref, ob_ref, oc_ref, od_ref)):
        cols = slice(p * MIX_W, (p + 1) * MIX_W)
        y = _head_rms(r[...], gho_ref[:, cols], bd).astype(BF16)
        acc = acc + jnp.dot(y, wout_ref[cols, :], preferred_element_type=F32)
    x1_ref[...] = acc
    h2_ref[...] = _row_rms(acc, gffn_ref[...]).astype(BF16)


def _mixer_out(x2d, oa, ob, oc, od, gho, wout, gffn, bd):
    t, d = x2d.shape
    full = lambda a: pl.BlockSpec(a.shape, lambda i: (0,) * a.ndim)
    tok = lambda w: pl.BlockSpec((TM, w), lambda i: (i, 0))
    return pl.pallas_call(
        _mixer_out_kernel,
        grid=(t // TM,),
        in_specs=[tok(d)] + [tok(MIX_W)] * 4 + [full(gho), full(wout), full(gffn), full(bd)],
        out_specs=[tok(d), tok(d)],
        out_shape=[jax.ShapeDtypeStruct((t, d), F32), jax.ShapeDtypeStruct((t, d), BF16)],
        compiler_params=_params("arbitrary"),
    )(x2d, oa, ob, oc, od, gho, wout, gffn, bd)


def _top16_rows(s):
    kidx = lax.broadcasted_iota(jnp.int32, s.shape, 0).astype(F32)
    rank = jnp.full(s.shape, float(PEER_TOPK), F32)
    vals = []
    for r in range(PEER_TOPK):
        m = jnp.max(s, axis=0, keepdims=True)
        first = jnp.min(jnp.where(s == m, kidx, float(PEER_N_KEYS)), axis=0, keepdims=True)
        hit = kidx == first
        rank = jnp.where(hit, float(r), rank)
        s = jnp.where(hit, -jnp.inf, s)
        vals.append(m)
    return vals, rank


def _peer_route_kernel(h2_ref, wqt_ref, keys_ref, r2_ref, e2_ref, n_ref, e1_ref, st_ref):
    tm = h2_ref.shape[0]
    h2 = h2_ref[...]
    for hp in range(2 * PEER_HEADS):
        rows = slice(hp * PEER_N_KEYS, (hp + 1) * PEER_N_KEYS)
        qt = lax.dot_general(wqt_ref[rows, :], h2, _NT, preferred_element_type=F32)
        st_ref[hp] = jnp.dot(keys_ref[hp], qt.astype(BF16), preferred_element_type=F32)

    nlb = tm // LANES
    sub = lax.broadcasted_iota(jnp.int32, (8, LANES), 0).astype(F32)
    pos = jnp.concatenate([sub, sub + 8.0] + [sub + 16.0 * a for a in range(1, 8)] + [(sub + 8.0) * 16.0], axis=0)

    def body(it, carry):
        hd = it // nlb
        lo = pl.multiple_of((it % nlb) * LANES, LANES)
        s1 = st_ref[2 * hd, :, pl.ds(lo, LANES)]
        s2 = st_ref[2 * hd + 1, :, pl.ds(lo, LANES)]
        v1, rk1 = _top16_rows(s1)
        v2, rk2 = _top16_rows(s2)
        sub16 = lax.broadcasted_iota(jnp.int32, (PEER_TOPK, LANES), 0).astype(F32)
        sv1 = jnp.zeros((PEER_TOPK, LANES), F32)
        sv2 = jnp.zeros((PEER_TOPK, LANES), F32)
        for r in range(PEER_TOPK):
            sv1 = jnp.where(sub16 == float(r), v1[r], sv1)
            sv2 = jnp.where(sub16 == float(r), v2[r], sv2)
        cand = jnp.concatenate(
            [v1[0] + sv2[0:8], v1[0] + sv2[8:16]] + [v1[a] + sv2[0:8] for a in range(1, 8)] + [sv1[8:16] + v2[0]],
            axis=0)
        cnt = jnp.zeros((PEER_N_KEYS, LANES), F32)
        zsum = jnp.zeros((1, LANES), F32)
        top0 = v1[0] + v2[0]
        for r in range(PEER_TOPK):
            m = jnp.max(cand, axis=0, keepdims=True)
            pr = jnp.min(jnp.where(cand == m, pos, 4096.0), axis=0, keepdims=True)
            cand = jnp.where(pos == pr, -jnp.inf, cand)
            cnt = cnt + jnp.where(rk1 == jnp.floor(pr * (1.0 / 16.0)), 1.0, 0.0)
            zsum = zsum + jnp.exp(m - top0)
        inv = 1.0 / zsum
        r2_ref[hd, :, pl.ds(lo, LANES)] = rk2.astype(BF16)
        e2_ref[hd, :, pl.ds(lo, LANES)] = jnp.exp(s2 - v2[0]).astype(BF16)
        n_ref[hd, :, pl.ds(lo, LANES)] = cnt
        e1_ref[hd, :, pl.ds(lo, LANES)] = jnp.exp(s1 - v1[0]) * inv
        return carry

    lax.fori_loop(0, PEER_HEADS * nlb, body, 0)


def _peer_route(h2, wqt, keys):
    t, d = h2.shape
    tabs = lambda dt: jax.ShapeDtypeStruct((PEER_HEADS, PEER_N_KEYS, t), dt)
    tab_spec = pl.BlockSpec((PEER_HEADS, PEER_N_KEYS, TM), lambda i: (0, 0, i))
    return pl.pallas_call(
        _peer_route_kernel,
        grid=(t // TM,),
        in_specs=[pl.BlockSpec((TM, d), lambda i: (i, 0)),
                  pl.BlockSpec(wqt.shape, lambda i: (0, 0)),
                  pl.BlockSpec(keys.shape, lambda i: (0, 0, 0))],
        out_specs=[tab_spec] * 4,
        out_shape=[tabs(BF16), tabs(BF16), tabs(F32), tabs(F32)],
        scratch_shapes=[pltpu.VMEM((2 * PEER_HEADS, PEER_N_KEYS, TM), F32)],
        compiler_params=_params("arbitrary"),
    )(h2, wqt, keys)


def _peer_ffn_kernel(h2_ref, u_ref, vt_ref, r2_ref, e2_ref, n_ref, e1_ref, x1_ref, o_ref, acc_ref, wt_ref):
    c = pl.program_id(1)
    tm = h2_ref.shape[0]
    ec = u_ref.shape[0]
    ni = ec // PEER_N_KEYS

    @pl.when(c == 0)
    def _():
        acc_ref[...] = jnp.zeros(acc_ref.shape, F32)

    at = lax.dot_general(u_ref[...], h2_ref[...], _NT, preferred_element_type=F32)
    irow = pl.multiple_of(c * ni, ni)
    for lb in range(tm // LANES):
        cols = slice(lb * LANES, (lb + 1) * LANES)
        nblk = [n_ref[hd, pl.ds(irow, ni), cols].astype(BF16) for hd in range(PEER_HEADS)]
        e1blk = [e1_ref[hd, pl.ds(irow, ni), cols].astype(BF16) for hd in range(PEER_HEADS)]
        for ii in range(ni):
            rows = slice(ii * PEER_N_KEYS, (ii + 1) * PEER_N_KEYS)
            gw = jnp.zeros((PEER_N_KEYS, LANES), BF16)
            for hd in range(PEER_HEADS):
                nrow = nblk[hd][ii:ii + 1, :]
                e1row = e1blk[hd][ii:ii + 1, :]
                margin = nrow - r2_ref[hd, :, cols]
                gw = gw + jnp.maximum(jnp.minimum(e1row * e2_ref[hd, :, cols], margin), 0.0)
            wt_ref[rows, cols] = jax.nn.gelu(at[rows, cols]).astype(BF16) * gw
    acc_ref[...] += jnp.dot(vt_ref[...], wt_ref[...], preferred_element_type=F32)

    @pl.when(c == pl.num_programs(1) - 1)
    def _():
        o_ref[...] = x1_ref[...] + acc_ref[...].T


def _peer_ffn(h2, u, vt, r2, e2, n, e1, x1):
    t, d = h2.shape
    ne = u.shape[0]
    tab_spec = pl.BlockSpec((PEER_HEADS, PEER_N_KEYS, TM), lambda i, c: (0, 0, i))
    return pl.pallas_call(
        _peer_ffn_kernel,
        grid=(t // TM, ne // PEER_EC),
        in_specs=[pl.BlockSpec((TM, d), lambda i, c: (i, 0)),
                  pl.BlockSpec((PEER_EC, d), lambda i, c: (c, 0)),
                  pl.BlockSpec((d, PEER_EC), lambda i, c: (0, c)),
                  tab_spec, tab_spec, tab_spec, tab_spec,
                  pl.BlockSpec((TM, d), lambda i, c: (i, 0))],
        out_specs=pl.BlockSpec((TM, d), lambda i, c: (i, 0)),
        out_shape=jax.ShapeDtypeStruct((t, d), F32),
        scratch_shapes=[pltpu.VMEM((d, TM), F32), pltpu.VMEM((PEER_EC, TM), BF16)],
        compiler_params=_params("arbitrary", "arbitrary"),
    )(h2, u, vt, r2, e2, n, e1, x1)


def _tile_heads(g):
    return jnp.tile(g, HEADS_PER_MIXER)[None, :]


def kernel(x, g_mix_norm, w_in, w_sgu, b_sgu, g_sgu_v, g_q_dil, g_k_dil, conv_w, g_q_sb, g_k_sb, g_head_out,
           w_out, g_ffn_norm, w_peer_q, peer_sub_keys, peer_u, peer_v):
    b, s, d = x.shape
    depth = w_in.shape[0]
    assert d == 4 * MIX_W and s % TM == 0 and s >= DIL_WIN
    assert w_in.shape[2] == N_IN_PIECES * MIX_W

    hh = np.arange(MIX_W) // HEAD_DIM
    bd = jnp.asarray(hh[:, None] == hh[None, :], BF16)
    lc = jnp.asarray(_dilated_logcount_table())
    umat = jnp.asarray(_suffix_sum_matrix(), BF16)

    x2d = x.reshape(b * s, d)
    for l in range(depth):
        wsgu_cat = w_sgu[l].transpose(1, 0, 2).reshape(SGU_CHUNK, HEADS_PER_MIXER * SGU_CHUNK)
        bsgu_tile = jnp.repeat(b_sgu[l].T, HEAD_DIM, axis=1)
        oa, oc, dq, dk, dv, bq, bk, bv = _mixer_in(
            x2d, g_mix_norm[l][None, :], w_in[l].astype(BF16), wsgu_cat, bsgu_tile, g_sgu_v[l][None, :],
            _tile_heads(g_q_dil[l]), _tile_heads(g_k_dil[l]), conv_w[l], _tile_heads(g_q_sb[l]),
            _tile_heads(g_k_sb[l]), bd, seq=s)
        r3 = lambda a: a.reshape(b, s, MIX_W)
        ob = _dilated(r3(dq), r3(dk), r3(dv), lc).reshape(b * s, MIX_W)
        od = _stickbreak(r3(bq), r3(bk), r3(bv), umat).reshape(b * s, MIX_W)
        x1, h2 = _mixer_out(x2d, oa, ob, oc, od, g_head_out[l][None, :], w_out[l].astype(BF16),
                            g_ffn_norm[l][None, :], bd)
        wqt = w_peer_q[l].T.astype(BF16)
        keys = peer_sub_keys[l].reshape(2 * PEER_HEADS, PEER_N_KEYS, -1).astype(BF16)
        r2, e2, n, e1 = _peer_route(h2, wqt, keys)
        x2d = _peer_ffn(h2, peer_u[l].astype(BF16), peer_v[l].T.astype(BF16), r2, e2, n, e1, x1)
    return x2d.reshape(b, s, d)
```

```python
import functools

import numpy as np
import jax
import jax.numpy as jnp
from jax import lax
from jax.experimental import pallas as pl
from jax.experimental.pallas import tpu as pltpu

HEAD_DIM = 64
MIX_W = 256
HEADS_PER_MIXER = MIX_W // HEAD_DIM
N_IN_PIECES = 11
SGU_CHUNK = 128
RMS_EPS = 1e-6
NEG = -1e30
LOG2E = 1.4426950408889634
PEER_HEADS = 8
PEER_N_KEYS = 128
PEER_TOPK = 16
ALIBI_SLOPES = tuple(2.0 ** (-8.0 * (i + 1) / HEADS_PER_MIXER) for i in range(HEADS_PER_MIXER))

LANES = 128
TM = 512
TQ = 256
DIL_MAX_DIST = 2048
DIL_NWIN = DIL_MAX_DIST // TQ + 1
DIL_WIN = DIL_NWIN * TQ
PEER_EC = 2048
PEER_SUB = 512
VMEM_LIMIT = 56 * 1024 * 1024

F32 = jnp.float32
BF16 = jnp.bfloat16
_NT = (((1,), (1,)), ((), ()))


def _params(*sem, flags=None):
    return pltpu.CompilerParams(dimension_semantics=sem, vmem_limit_bytes=VMEM_LIMIT, flags=flags)


def _head_sumsq(x2, bd):
    hi = x2.astype(BF16)
    lo = (x2 - hi.astype(F32)).astype(BF16)
    return (jnp.dot(hi, bd, preferred_element_type=F32) + jnp.dot(lo, bd, preferred_element_type=F32))


def _head_rms(x, g, bd):
    ss = _head_sumsq(x * x, bd)
    return x * lax.rsqrt(ss * (1.0 / HEAD_DIM) + RMS_EPS) * g


def _row_rms(x, g):
    ms = jnp.mean(x * x, axis=-1, keepdims=True)
    return x * lax.rsqrt(ms + RMS_EPS) * g


def _mixer_in_kernel(x_ref, gmix_ref, win_ref, wsgu_ref, bsgu_ref, gv_ref, gqd_ref, gkd_ref, convw_ref,
                     gqs_ref, gks_ref, bd_ref,
                     oa_ref, oc_ref, dq_ref, dk_ref, dv_ref, bq_ref, bk_ref, bv_ref,
                     ybuf_ref, *, tiles_per_seq):
    i = pl.program_id(0)
    tm = x_ref.shape[0]
    h = _row_rms(x_ref[...], gmix_ref[...]).astype(BF16)
    bd = bd_ref[...]
    scale = HEAD_DIM ** -0.5

    def piece(p):
        return jnp.dot(h, win_ref[:, p * MIX_W:(p + 1) * MIX_W], preferred_element_type=F32)

    u = jax.nn.gelu(piece(0))
    v = _head_rms(jax.nn.gelu(piece(1)), gv_ref[...], bd)
    t_idx = lax.broadcasted_iota(jnp.int32, (SGU_CHUNK, HEADS_PER_MIXER * SGU_CHUNK), 0)
    s_idx = lax.broadcasted_iota(jnp.int32, (SGU_CHUNK, HEADS_PER_MIXER * SGU_CHUNK), 1) & (SGU_CHUNK - 1)
    wcat = jnp.where(s_idx <= t_idx, wsgu_ref[...], 0.0).astype(BF16)
    lane_head = lax.broadcasted_iota(jnp.int32, (SGU_CHUNK, MIX_W), 1) >> 6
    bias = bsgu_ref[...]
    for c in range(tm // SGU_CHUNK):
        rows = slice(c * SGU_CHUNK, (c + 1) * SGU_CHUNK)
        vc = v[rows, :]
        vstack = jnp.concatenate(
            [jnp.where(lane_head == hh, vc, 0.0).astype(BF16) for hh in range(HEADS_PER_MIXER)], axis=0)
        mixed = jnp.dot(wcat, vstack, preferred_element_type=F32) + bias
        oa_ref[rows, :] = u[rows, :] * mixed

    dq_ref[...] = (_head_rms(piece(2), gqd_ref[...], bd) * scale).astype(BF16)
    dk_ref[...] = _head_rms(piece(3), gkd_ref[...], bd).astype(BF16)
    dv_ref[...] = piece(4).astype(BF16)
    bq_ref[...] = (_head_rms(piece(8), gqs_ref[...], bd) * scale).astype(BF16)
    bk_ref[...] = _head_rms(piece(9), gks_ref[...], bd).astype(BF16)
    bv_ref[...] = piece(10).astype(BF16)

    cb = piece(5)
    y = piece(6) * piece(7)

    @pl.when(i % tiles_per_seq == 0)
    def _():
        ybuf_ref[0:8, :] = jnp.zeros((8, MIX_W), F32)

    ybuf_ref[8:8 + tm, :] = y
    y1 = ybuf_ref[7:7 + tm, :]
    y2 = ybuf_ref[6:6 + tm, :]
    w = convw_ref[...]
    oc_ref[...] = cb * (w[0:1, :] * y2 + w[1:2, :] * y1 + w[2:3, :] * y)
    ybuf_ref[0:8, :] = y[tm - 8:tm, :]


def _mixer_in(x2d, gmix, win, wsgu_cat, bsgu_tile, gv, gqd, gkd, convw, gqs, gks, bd, *, seq):
    t = x2d.shape[0]
    d = x2d.shape[1]
    full = lambda a: pl.BlockSpec(a.shape, lambda i: (0,) * a.ndim)
    tok = lambda w: pl.BlockSpec((TM, w), lambda i: (i, 0))
    consts = (gmix, win, wsgu_cat, bsgu_tile, gv, gqd, gkd, convw, gqs, gks, bd)
    out_shape = ([jax.ShapeDtypeStruct((t, MIX_W), F32)] * 2 + [jax.ShapeDtypeStruct((t, MIX_W), BF16)] * 6)
    return pl.pallas_call(
        functools.partial(_mixer_in_kernel, tiles_per_seq=seq // TM),
        grid=(t // TM,),
        in_specs=[tok(d)] + [full(a) for a in consts],
        out_specs=[tok(MIX_W)] * 8,
        out_shape=out_shape,
        scratch_shapes=[pltpu.VMEM((TM + 8, MIX_W), F32)],
        compiler_params=_params("arbitrary"),
    )(x2d, *consts)


def _dilated_logcount_table():
    n = 2 * DIL_NWIN - 1
    qi = np.arange(TQ)[:, None]
    kj = np.arange(TQ)[None, :]
    tiles = []
    for u in range(n):
        off = DIL_NWIN - 1 - u
        dd = off * TQ + qi - kj
        cnt = ((dd >= 0) & (dd <= 128)).astype(np.int64)
        cnt += ((dd >= 0) & (dd <= 512) & (dd % 4 == 0))
        cnt += ((dd >= 0) & (dd <= 2048) & (dd % 16 == 0))
        tiles.append(np.where(cnt > 0, np.log(np.maximum(cnt, 1).astype(np.float64)), NEG))
    return np.concatenate(tiles, axis=1).astype(np.float32)


def _dilated_kernel(q_ref, k_ref, v_ref, lc_ref, o_ref):
    qb = pl.program_id(1)
    o0 = jnp.minimum(qb, DIL_NWIN - 1)
    kstart = pl.multiple_of((qb - o0) * TQ, TQ)
    cstart = pl.multiple_of((DIL_NWIN - 1 - o0) * TQ, TQ)
    lcw = lc_ref[:, pl.ds(cstart, DIL_WIN)]
    qi = lax.broadcasted_iota(jnp.int32, (TQ, DIL_WIN), 0)
    col = lax.broadcasted_iota(jnp.int32, (TQ, DIL_WIN), 1)
    dist = (o0 * TQ + qi - col).astype(F32)
    q = q_ref[0]
    kw = k_ref[0, pl.ds(kstart, DIL_WIN), :]
    vw = v_ref[0, pl.ds(kstart, DIL_WIN), :]
    lane_head = lax.broadcasted_iota(jnp.int32, (TQ, LANES), 1) >> 6
    for pair in range(2):
        cols = slice(pair * LANES, (pair + 1) * LANES)
        qp, kp, vp = q[:, cols], kw[:, cols], vw[:, cols]
        outs = []
        for hh in range(2):
            qm = jnp.where(lane_head == hh, qp, jnp.zeros_like(qp))
            s = lax.dot_general(qm, kp, _NT, preferred_element_type=F32)
            s = s + (lcw - ALIBI_SLOPES[2 * pair + hh] * dist)
            m = jnp.max(s, axis=-1, keepdims=True)
            p = jnp.exp(s - m)
            l = jnp.sum(p, axis=-1, keepdims=True)
            o = jnp.dot(p.astype(BF16), vp, preferred_element_type=F32)
            outs.append(o / l)
        o_ref[0, :, cols] = jnp.where(lane_head == 0, outs[0], outs[1])


def _dilated(q, k, v, lc):
    b, s, w = q.shape
    return pl.pallas_call(
        _dilated_kernel,
        grid=(b, s // TQ),
        in_specs=[pl.BlockSpec((1, TQ, w), lambda bi, qi: (bi, qi, 0)),
                  pl.BlockSpec((1, s, w), lambda bi, qi: (bi, 0, 0)),
                  pl.BlockSpec((1, s, w), lambda bi, qi: (bi, 0, 0)),
                  pl.BlockSpec(lc.shape, lambda bi, qi: (0, 0))],
        out_specs=pl.BlockSpec((1, TQ, w), lambda bi, qi: (bi, qi, 0)),
        out_shape=jax.ShapeDtypeStruct((b, s, w), F32),
        compiler_params=_params("arbitrary", "arbitrary"),
    )(q, k, v, lc)


def _suffix_sum_matrix():
    j = np.arange(TQ)[:, None]
    s = np.arange(TQ)[None, :]
    return np.concatenate([(j >= s), np.ones((TQ, LANES), bool)], axis=1).astype(np.float32)


def _stickbreak_kernel(q_ref, k_ref, v_ref, u_ref, o_ref, acc_ref, run_ref):
    qb = pl.program_id(1)
    q = q_ref[0]
    umat = u_ref[...]
    acc_ref[...] = jnp.zeros(acc_ref.shape, F32)
    run_ref[...] = jnp.zeros(run_ref.shape, F32)
    lane_head = lax.broadcasted_iota(jnp.int32, (TQ, LANES), 1) >> 6
    qm = []
    for pair in range(2):
        qp = q[:, pair * LANES:(pair + 1) * LANES]
        for hh in range(2):
            qm.append(jnp.where(lane_head == hh, qp, jnp.zeros_like(qp)))
    row = lax.broadcasted_iota(jnp.int32, (TQ, TQ), 0)
    colk = lax.broadcasted_iota(jnp.int32, (TQ, TQ), 1)
    causal = colk < row

    def tile(kb, diagonal):
        koff = pl.multiple_of(kb * TQ, TQ)
        kt = k_ref[0, pl.ds(koff, TQ), :]
        vt = v_ref[0, pl.ds(koff, TQ), :]
        for pair in range(2):
            cols = slice(pair * LANES, (pair + 1) * LANES)
            for hh in range(2):
                hd = 2 * pair + hh
                y = lax.dot_general(qm[hd], kt[:, cols], _NT, preferred_element_type=F32) * LOG2E
                nl = jnp.maximum(y, 0.0) + jnp.log2(1.0 + jnp.exp2(-jnp.abs(y)))
                if diagonal:
                    nl = jnp.where(causal, nl, 0.0)
                cs = jnp.dot(nl.astype(BF16), umat, preferred_element_type=F32)
                run = run_ref[hd]
                csum = cs[:, :TQ] + jnp.concatenate([run, run], axis=1)
                a = jnp.exp2(y - csum)
                if diagonal:
                    a = jnp.where(causal, a, 0.0)
                pv = jnp.dot(a.astype(BF16), vt[:, cols], preferred_element_type=F32)
                acc_ref[:, cols] += jnp.where(lane_head == hh, pv, 0.0)
                run_ref[hd] = run + cs[:, TQ:]

    tile(qb, True)

    def body(j, carry):
        tile(qb - 1 - j, False)
        return carry

    lax.fori_loop(0, qb, body, 0)
    o_ref[0] = acc_ref[...]


def _stickbreak(q, k, v, umat):
    b, s, w = q.shape
    return pl.pallas_call(
        _stickbreak_kernel,
        grid=(b, s // TQ),
        in_specs=[pl.BlockSpec((1, TQ, w), lambda bi, qi: (bi, qi, 0)),
                  pl.BlockSpec((1, s, w), lambda bi, qi: (bi, 0, 0)),
                  pl.BlockSpec((1, s, w), lambda bi, qi: (bi, 0, 0)),
                  pl.BlockSpec(umat.shape, lambda bi, qi: (0, 0))],
        out_specs=pl.BlockSpec((1, TQ, w), lambda bi, qi: (bi, qi, 0)),
        out_shape=jax.ShapeDtypeStruct((b, s, w), F32),
        scratch_shapes=[pltpu.VMEM((TQ, w), F32), pltpu.VMEM((HEADS_PER_MIXER, TQ, LANES), F32)],
        compiler_params=_params("arbitrary", "arbitrary"),
    )(q, k, v, umat)


def _mixer_out_kernel(x_ref, oa_ref, ob_ref, oc_ref, od_ref, gho_ref, wout_ref, gffn_ref, bd_ref,
                      x1_ref, h2_ref):
    bd = bd_ref[...]
    acc = x_ref[...]
    for p, r in enumerate((oa_ref, ob_ref, oc_ref, od_ref)):
        cols = slice(p * MIX_W, (p + 1) * MIX_W)
        y = _head_rms(r[...], gho_ref[:, cols], bd).astype(BF16)
        acc = acc + jnp.dot(y, wout_ref[cols, :], preferred_element_type=F32)
    x1_ref[...] = acc
    h2_ref[...] = _row_rms(acc, gffn_ref[...]).astype(BF16)


def _mixer_out(x2d, oa, ob, oc, od, gho, wout, gffn, bd):
    t, d = x2d.shape
    full = lambda a: pl.BlockSpec(a.shape, lambda i: (0,) * a.ndim)
    tok = lambda w: pl.BlockSpec((TM, w), lambda i: (i, 0))
    return pl.pallas_call(
        _mixer_out_kernel,
        grid=(t // TM,),
        in_specs=[tok(d)] + [tok(MIX_W)] * 4 + [full(gho), full(wout), full(gffn), full(bd)],
        out_specs=[tok(d), tok(d)],
        out_shape=[jax.ShapeDtypeStruct((t, d), F32), jax.ShapeDtypeStruct((t, d), BF16)],
        compiler_params=_params("arbitrary"),
    )(x2d, oa, ob, oc, od, gho, wout, gffn, bd)


def _top16_rows(s):
    kidx = lax.broadcasted_iota(jnp.int32, s.shape, 0).astype(F32)
    rank = jnp.full(s.shape, float(PEER_TOPK), F32)
    vals = []
    for r in range(PEER_TOPK):
        m = jnp.max(s, axis=0, keepdims=True)
        first = jnp.min(jnp.where(s == m, kidx, float(PEER_N_KEYS)), axis=0, keepdims=True)
        hit = kidx == first
        rank = jnp.where(hit, float(r), rank)
        s = jnp.where(hit, -jnp.inf, s)
        vals.append(m)
    return vals, rank


def _top16_rows_unique(s):
    rank = jnp.full(s.shape, float(PEER_TOPK), F32)
    vals = []
    for r in range(PEER_TOPK):
        m = jnp.max(s, axis=0, keepdims=True)
        hit = s == m
        rank = jnp.where(hit, float(r), rank)
        s = jnp.where(hit, -jnp.inf, s)
        vals.append(m)
    hits = jnp.sum(jnp.where(rank < float(PEER_TOPK), 1.0, 0.0), axis=0, keepdims=True)
    return vals, rank, hits


def _peer_route_kernel(h2_ref, wqt_ref, keys_ref, r2_ref, e2_ref, n_ref, e1_ref, st_ref):
    tm = h2_ref.shape[0]
    qt = lax.dot_general(wqt_ref[...], h2_ref[...], _NT, preferred_element_type=F32).astype(BF16)
    for hp in range(2 * PEER_HEADS):
        rows = slice(hp * PEER_N_KEYS, (hp + 1) * PEER_N_KEYS)
        st_ref[hp] = jnp.dot(keys_ref[hp], qt[rows, :], preferred_element_type=F32)

    nlb = tm // LANES
    sub = lax.broadcasted_iota(jnp.int32, (8, LANES), 0).astype(F32)
    pos = jnp.concatenate([sub, sub + 8.0] + [sub + 16.0 * a for a in range(1, 8)] + [(sub + 8.0) * 16.0], axis=0)
    sub16 = lax.broadcasted_iota(jnp.int32, (PEER_TOPK, LANES), 0).astype(F32)

    def candidates(v1, v2):
        sv1 = jnp.zeros((PEER_TOPK, LANES), F32)
        sv2 = jnp.zeros((PEER_TOPK, LANES), F32)
        for r in range(PEER_TOPK):
            sv1 = jnp.where(sub16 == float(r), v1[r], sv1)
            sv2 = jnp.where(sub16 == float(r), v2[r], sv2)
        return jnp.concatenate(
            [v1[0] + sv2[0:8], v1[0] + sv2[8:16]] + [v1[a] + sv2[0:8] for a in range(1, 8)] + [sv1[8:16] + v2[0]],
            axis=0)

    def store(hd, lb, s1, s2, v1, v2, rk2, cnt, zsum):
        r2_ref[hd, lb] = rk2.astype(BF16)
        e2_ref[hd, lb] = jnp.exp(s2 - v2[0]).astype(BF16)
        n_ref[hd, lb] = cnt
        e1_ref[hd, lb] = jnp.exp(s1 - v1[0]) * (1.0 / zsum)

    def body(it, carry):
        hd = it // nlb
        lb = it % nlb
        lo = pl.multiple_of(lb * LANES, LANES)
        s1 = st_ref[2 * hd, :, pl.ds(lo, LANES)]
        s2 = st_ref[2 * hd + 1, :, pl.ds(lo, LANES)]

        v1, rk1, hits1 = _top16_rows_unique(s1)
        v2, rk2, hits2 = _top16_rows_unique(s2)
        cand = candidates(v1, v2)
        sel = jnp.zeros(cand.shape, F32)
        zsum = jnp.zeros((1, LANES), F32)
        top0 = v1[0] + v2[0]
        for r in range(PEER_TOPK):
            m = jnp.max(cand, axis=0, keepdims=True)
            hit = cand == m
            cand = jnp.where(hit, -jnp.inf, cand)
            sel = jnp.where(hit, 1.0, sel)
            zsum = zsum + jnp.exp(m - top0)
        na = [jnp.sum(sel[0:16], axis=0, keepdims=True)]
        na += [jnp.sum(sel[8 + 8 * a:16 + 8 * a], axis=0, keepdims=True) for a in range(1, 8)]
        last = sel[72:80]
        cnt = jnp.zeros((PEER_N_KEYS, LANES), F32)
        for a in range(8):
            cnt = jnp.where(rk1 == float(a), na[a], cnt)
        for a in range(8, PEER_TOPK):
            cnt = jnp.where(rk1 == float(a), last[a - 8:a - 7], cnt)
        store(hd, lb, s1, s2, v1, v2, rk2, cnt, zsum)
        picked = jnp.sum(sel, axis=0, keepdims=True)
        worst = jnp.max(jnp.maximum(jnp.maximum(hits1, hits2), picked))

        @pl.when(worst > PEER_TOPK + 0.5)
        def _():
            v1, rk1 = _top16_rows(s1)
            v2, rk2 = _top16_rows(s2)
            cand = candidates(v1, v2)
            cnt = jnp.zeros((PEER_N_KEYS, LANES), F32)
            zsum = jnp.zeros((1, LANES), F32)
            top0 = v1[0] + v2[0]
            for r in range(PEER_TOPK):
                m = jnp.max(cand, axis=0, keepdims=True)
                pr = jnp.min(jnp.where(cand == m, pos, 4096.0), axis=0, keepdims=True)
                cand = jnp.where(pos == pr, -jnp.inf, cand)
                cnt = cnt + jnp.where(rk1 == jnp.floor(pr * (1.0 / 16.0)), 1.0, 0.0)
                zsum = zsum + jnp.exp(m - top0)
            store(hd, lb, s1, s2, v1, v2, rk2, cnt, zsum)

        return carry

    lax.fori_loop(0, PEER_HEADS * nlb, body, 0)


def _peer_route(h2, wqt, keys):
    t, d = h2.shape
    tabs = lambda dt: jax.ShapeDtypeStruct((PEER_HEADS, t // LANES, PEER_N_KEYS, LANES), dt)
    tab_spec = pl.BlockSpec((PEER_HEADS, TM // LANES, PEER_N_KEYS, LANES), lambda i: (0, i, 0, 0))
    return pl.pallas_call(
        _peer_route_kernel,
        grid=(t // TM,),
        in_specs=[pl.BlockSpec((TM, d), lambda i: (i, 0)),
                  pl.BlockSpec(wqt.shape, lambda i: (0, 0)),
                  pl.BlockSpec(keys.shape, lambda i: (0, 0, 0))],
        out_specs=[tab_spec] * 4,
        out_shape=[tabs(BF16), tabs(BF16), tabs(F32), tabs(F32)],
        scratch_shapes=[pltpu.VMEM((2 * PEER_HEADS, PEER_N_KEYS, TM), F32)],
        compiler_params=_params("arbitrary"),
    )(h2, wqt, keys)


def _peer_ffn_kernel(h2_ref, u_ref, vt_ref, r2_ref, e2_ref, n_ref, e1_ref, x1_ref, o_ref,
                     acc_ref, at_ref, wt_ref):
    c = pl.program_id(1)
    tm = h2_ref.shape[0]
    ec = u_ref.shape[0]
    nsub = ec // PEER_SUB
    i_per_sub = PEER_SUB // PEER_N_KEYS
    assert 2 * i_per_sub == 8 and nsub % 2 == 0

    @pl.when(c == 0)
    def _():
        acc_ref[...] = jnp.zeros(acc_ref.shape, F32)

    def stage_b(sb, parity, wt_dst):
        grp = pl.ds(pl.multiple_of((c * nsub + sb - parity) * i_per_sub, 2 * i_per_sub), 2 * i_per_sub)
        for k in range(i_per_sub):
            r = parity * i_per_sub + k
            rows = slice(k * PEER_N_KEYS, (k + 1) * PEER_N_KEYS)
            arow = pl.ds(pl.multiple_of((sb * i_per_sub + k) * PEER_N_KEYS, PEER_N_KEYS), PEER_N_KEYS)
            for lb in range(tm // LANES):
                cols = slice(lb * LANES, (lb + 1) * LANES)
                gw = jnp.zeros((PEER_N_KEYS, LANES), BF16)
                for hd in range(PEER_HEADS):
                    nrow = n_ref[hd, lb, grp, :].astype(BF16)[r:r + 1, :]
                    e1row = e1_ref[hd, lb, grp, :].astype(BF16)[r:r + 1, :]
                    margin = nrow - r2_ref[hd, lb]
                    gw = gw + jnp.maximum(jnp.minimum(e1row * e2_ref[hd, lb], margin), 0.0)
                wt_dst[rows, cols] = jax.nn.gelu(at_ref[arow, cols]).astype(BF16) * gw

    def stage_c(sb, wt_src):
        cols = pl.ds(pl.multiple_of(sb * PEER_SUB, PEER_SUB), PEER_SUB)
        acc_ref[...] += jnp.dot(vt_ref[:, cols], wt_src[...], preferred_element_type=F32)

    at_ref[...] = lax.dot_general(u_ref[...], h2_ref[...], _NT, preferred_element_type=F32)
    for sb in range(nsub):
        stage_b(sb, sb % 2, wt_ref.at[pl.ds(sb * PEER_SUB, PEER_SUB)])
        if sb > 0:
            stage_c(sb - 1, wt_ref.at[pl.ds((sb - 1) * PEER_SUB, PEER_SUB)])
    stage_c(nsub - 1, wt_ref.at[pl.ds((nsub - 1) * PEER_SUB, PEER_SUB)])

    @pl.when(c == pl.num_programs(1) - 1)
    def _():
        o_ref[...] = x1_ref[...] + acc_ref[...].T


def _peer_ffn(h2, u, vt, r2, e2, n, e1, x1):
    t, d = h2.shape
    ne = u.shape[0]
    tab_spec = pl.BlockSpec((PEER_HEADS, TM // LANES, PEER_N_KEYS, LANES), lambda i, c: (0, i, 0, 0))
    return pl.pallas_call(
        _peer_ffn_kernel,
        grid=(t // TM, ne // PEER_EC),
        in_specs=[pl.BlockSpec((TM, d), lambda i, c: (i, 0)),
                  pl.BlockSpec((PEER_EC, d), lambda i, c: (c, 0)),
                  pl.BlockSpec((d, PEER_EC), lambda i, c: (0, c)),
                  tab_spec, tab_spec, tab_spec, tab_spec,
                  pl.BlockSpec((TM, d), lambda i, c: (i, 0))],
        out_specs=pl.BlockSpec((TM, d), lambda i, c: (i, 0)),
        out_shape=jax.ShapeDtypeStruct((t, d), F32),
        scratch_shapes=[pltpu.VMEM((d, TM), F32), pltpu.VMEM((PEER_EC, TM), F32), pltpu.VMEM((PEER_EC, TM), BF16)],
        compiler_params=_params("arbitrary", "arbitrary"),
    )(h2, u, vt, r2, e2, n, e1, x1)


def _tile_heads(g):
    return jnp.tile(g, HEADS_PER_MIXER)[None, :]


def kernel(x, g_mix_norm, w_in, w_sgu, b_sgu, g_sgu_v, g_q_dil, g_k_dil, conv_w, g_q_sb, g_k_sb, g_head_out,
           w_out, g_ffn_norm, w_peer_q, peer_sub_keys, peer_u, peer_v):
    b, s, d = x.shape
    depth = w_in.shape[0]
    assert d == 4 * MIX_W and s % TM == 0 and s >= DIL_WIN
    assert w_in.shape[2] == N_IN_PIECES * MIX_W

    hh = np.arange(MIX_W) // HEAD_DIM
    bd = jnp.asarray(hh[:, None] == hh[None, :], BF16)
    lc = jnp.asarray(_dilated_logcount_table())
    umat = jnp.asarray(_suffix_sum_matrix(), BF16)

    x2d = x.reshape(b * s, d)
    for l in range(depth):
        wsgu_cat = w_sgu[l].transpose(1, 0, 2).reshape(SGU_CHUNK, HEADS_PER_MIXER * SGU_CHUNK)
        bsgu_tile = jnp.repeat(b_sgu[l].T, HEAD_DIM, axis=1)
        oa, oc, dq, dk, dv, bq, bk, bv = _mixer_in(
            x2d, g_mix_norm[l][None, :], w_in[l].astype(BF16), wsgu_cat, bsgu_tile, g_sgu_v[l][None, :],
            _tile_heads(g_q_dil[l]), _tile_heads(g_k_dil[l]), conv_w[l], _tile_heads(g_q_sb[l]),
            _tile_heads(g_k_sb[l]), bd, seq=s)
        r3 = lambda a: a.reshape(b, s, MIX_W)
        ob = _dilated(r3(dq), r3(dk), r3(dv), lc).reshape(b * s, MIX_W)
        od = _stickbreak(r3(bq), r3(bk), r3(bv), umat).reshape(b * s, MIX_W)
        x1, h2 = _mixer_out(x2d, oa, ob, oc, od, g_head_out[l][None, :], w_out[l].astype(BF16),
                            g_ffn_norm[l][None, :], bd)
        wqt = w_peer_q[l].T.astype(BF16)
        keys = peer_sub_keys[l].reshape(2 * PEER_HEADS, PEER_N_KEYS, -1).astype(BF16)
        r2, e2, n, e1 = _peer_route(h2, wqt, keys)
        x2d = _peer_ffn(h2, peer_u[l].astype(BF16), peer_v[l].T.astype(BF16), r2, e2, n, e1, x1)
    return x2d.reshape(b, s, d)
```

```python
import functools

import numpy as np
import jax
import jax.numpy as jnp
from jax import lax
from jax.experimental import pallas as pl
from jax.experimental.pallas import tpu as pltpu

HEAD_DIM = 64
MIX_W = 256
HEADS_PER_MIXER = MIX_W // HEAD_DIM
N_IN_PIECES = 11
SGU_CHUNK = 128
RMS_EPS = 1e-6
NEG = -1e30
LOG2E = 1.4426950408889634
PEER_HEADS = 8
PEER_N_KEYS = 128
PEER_TOPK = 16
ALIBI_SLOPES = tuple(2.0 ** (-8.0 * (i + 1) / HEADS_PER_MIXER) for i in range(HEADS_PER_MIXER))

LANES = 128
TM = 512
TQ = 256
DIL_MAX_DIST = 2048
DIL_NWIN = DIL_MAX_DIST // TQ + 1
DIL_WIN = DIL_NWIN * TQ
PEER_EC = 2048
ROUTE_BLOCKS_PER_TRIP = 4
PEER_GROUP = 8
VMEM_LIMIT = 56 * 1024 * 1024

F32 = jnp.float32
BF16 = jnp.bfloat16
_NT = (((1,), (1,)), ((), ()))


def _params(*sem, flags=None):
    return pltpu.CompilerParams(dimension_semantics=sem, vmem_limit_bytes=VMEM_LIMIT, flags=flags)


def _head_sumsq(x2, bd):
    hi = x2.astype(BF16)
    lo = (x2 - hi.astype(F32)).astype(BF16)
    return (jnp.dot(hi, bd, preferred_element_type=F32) + jnp.dot(lo, bd, preferred_element_type=F32))


def _head_rms(x, g, bd):
    ss = _head_sumsq(x * x, bd)
    return x * lax.rsqrt(ss * (1.0 / HEAD_DIM) + RMS_EPS) * g


def _row_rms(x, g):
    ms = jnp.mean(x * x, axis=-1, keepdims=True)
    return x * lax.rsqrt(ms + RMS_EPS) * g


def _mixer_in_kernel(x_ref, gmix_ref, win_ref, wsgu_ref, bsgu_ref, gv_ref, gqd_ref, gkd_ref, convw_ref,
                     gqs_ref, gks_ref, bd_ref,
                     oa_ref, oc_ref, dq_ref, dk_ref, dv_ref, bq_ref, bk_ref, bv_ref,
                     ybuf_ref, *, tiles_per_seq):
    i = pl.program_id(0)
    tm = x_ref.shape[0]
    h = _row_rms(x_ref[...], gmix_ref[...]).astype(BF16)
    bd = bd_ref[...]
    scale = HEAD_DIM ** -0.5

    def piece(p):
        return jnp.dot(h, win_ref[:, p * MIX_W:(p + 1) * MIX_W], preferred_element_type=F32)

    u = jax.nn.gelu(piece(0))
    v = _head_rms(jax.nn.gelu(piece(1)), gv_ref[...], bd)
    t_idx = lax.broadcasted_iota(jnp.int32, (SGU_CHUNK, HEADS_PER_MIXER * SGU_CHUNK), 0)
    s_idx = lax.broadcasted_iota(jnp.int32, (SGU_CHUNK, HEADS_PER_MIXER * SGU_CHUNK), 1) & (SGU_CHUNK - 1)
    wcat = jnp.where(s_idx <= t_idx, wsgu_ref[...], 0.0).astype(BF16)
    lane_head = lax.broadcasted_iota(jnp.int32, (SGU_CHUNK, MIX_W), 1) >> 6
    bias = bsgu_ref[...]
    for c in range(tm // SGU_CHUNK):
        rows = slice(c * SGU_CHUNK, (c + 1) * SGU_CHUNK)
        vc = v[rows, :]
        vstack = jnp.concatenate(
            [jnp.where(lane_head == hh, vc, 0.0).astype(BF16) for hh in range(HEADS_PER_MIXER)], axis=0)
        mixed = jnp.dot(wcat, vstack, preferred_element_type=F32) + bias
        oa_ref[rows, :] = u[rows, :] * mixed

    dq_ref[...] = (_head_rms(piece(2), gqd_ref[...], bd) * scale).astype(BF16)
    dk_ref[...] = _head_rms(piece(3), gkd_ref[...], bd).astype(BF16)
    dv_ref[...] = piece(4).astype(BF16)
    bq_ref[...] = (_head_rms(piece(8), gqs_ref[...], bd) * scale).astype(BF16)
    bk_ref[...] = _head_rms(piece(9), gks_ref[...], bd).astype(BF16)
    bv_ref[...] = piece(10).astype(BF16)

    cb = piece(5)
    y = piece(6) * piece(7)

    @pl.when(i % tiles_per_seq == 0)
    def _():
        ybuf_ref[0:8, :] = jnp.zeros((8, MIX_W), F32)

    ybuf_ref[8:8 + tm, :] = y
    y1 = ybuf_ref[7:7 + tm, :]
    y2 = ybuf_ref[6:6 + tm, :]
    w = convw_ref[...]
    oc_ref[...] = cb * (w[0:1, :] * y2 + w[1:2, :] * y1 + w[2:3, :] * y)
    ybuf_ref[0:8, :] = y[tm - 8:tm, :]


def _mixer_in(x2d, gmix, win, wsgu_cat, bsgu_tile, gv, gqd, gkd, convw, gqs, gks, bd, *, seq):
    t = x2d.shape[0]
    d = x2d.shape[1]
    full = lambda a: pl.BlockSpec(a.shape, lambda i: (0,) * a.ndim)
    tok = lambda w: pl.BlockSpec((TM, w), lambda i: (i, 0))
    consts = (gmix, win, wsgu_cat, bsgu_tile, gv, gqd, gkd, convw, gqs, gks, bd)
    out_shape = ([jax.ShapeDtypeStruct((t, MIX_W), F32)] * 2 + [jax.ShapeDtypeStruct((t, MIX_W), BF16)] * 6)
    return pl.pallas_call(
        functools.partial(_mixer_in_kernel, tiles_per_seq=seq // TM),
        grid=(t // TM,),
        in_specs=[tok(d)] + [full(a) for a in consts],
        out_specs=[tok(MIX_W)] * 8,
        out_shape=out_shape,
        scratch_shapes=[pltpu.VMEM((TM + 8, MIX_W), F32)],
        compiler_params=_params("arbitrary"),
    )(x2d, *consts)


def _dilated_logcount_table():
    n = 2 * DIL_NWIN - 1
    qi = np.arange(TQ)[:, None]
    kj = np.arange(TQ)[None, :]
    tiles = []
    for u in range(n):
        off = DIL_NWIN - 1 - u
        dd = off * TQ + qi - kj
        cnt = ((dd >= 0) & (dd <= 128)).astype(np.int64)
        cnt += ((dd >= 0) & (dd <= 512) & (dd % 4 == 0))
        cnt += ((dd >= 0) & (dd <= 2048) & (dd % 16 == 0))
        tiles.append(np.where(cnt > 0, np.log(np.maximum(cnt, 1).astype(np.float64)), NEG))
    return np.concatenate(tiles, axis=1).astype(np.float32)


def _dilated_kernel(q_ref, k_ref, v_ref, lc_ref, o_ref):
    qb = pl.program_id(1)
    o0 = jnp.minimum(qb, DIL_NWIN - 1)
    kstart = pl.multiple_of((qb - o0) * TQ, TQ)
    cstart = pl.multiple_of((DIL_NWIN - 1 - o0) * TQ, TQ)
    lcw = lc_ref[:, pl.ds(cstart, DIL_WIN)]
    qi = lax.broadcasted_iota(jnp.int32, (TQ, DIL_WIN), 0)
    col = lax.broadcasted_iota(jnp.int32, (TQ, DIL_WIN), 1)
    dist = (o0 * TQ + qi - col).astype(F32)
    q = q_ref[0]
    kw = k_ref[0, pl.ds(kstart, DIL_WIN), :]
    vw = v_ref[0, pl.ds(kstart, DIL_WIN), :]
    lane_head = lax.broadcasted_iota(jnp.int32, (TQ, LANES), 1) >> 6
    for pair in range(2):
        cols = slice(pair * LANES, (pair + 1) * LANES)
        qp, kp, vp = q[:, cols], kw[:, cols], vw[:, cols]
        outs = []
        for hh in range(2):
            qm = jnp.where(lane_head == hh, qp, jnp.zeros_like(qp))
            s = lax.dot_general(qm, kp, _NT, preferred_element_type=F32)
            s = s + (lcw - ALIBI_SLOPES[2 * pair + hh] * dist)
            m = jnp.max(s, axis=-1, keepdims=True)
            p = jnp.exp(s - m)
            l = jnp.sum(p, axis=-1, keepdims=True)
            o = jnp.dot(p.astype(BF16), vp, preferred_element_type=F32)
            outs.append(o / l)
        o_ref[0, :, cols] = jnp.where(lane_head == 0, outs[0], outs[1])


def _dilated(q, k, v, lc):
    b, s, w = q.shape
    return pl.pallas_call(
        _dilated_kernel,
        grid=(b, s // TQ),
        in_specs=[pl.BlockSpec((1, TQ, w), lambda bi, qi: (bi, qi, 0)),
                  pl.BlockSpec((1, s, w), lambda bi, qi: (bi, 0, 0)),
                  pl.BlockSpec((1, s, w), lambda bi, qi: (bi, 0, 0)),
                  pl.BlockSpec(lc.shape, lambda bi, qi: (0, 0))],
        out_specs=pl.BlockSpec((1, TQ, w), lambda bi, qi: (bi, qi, 0)),
        out_shape=jax.ShapeDtypeStruct((b, s, w), F32),
        compiler_params=_params("arbitrary", "arbitrary"),
    )(q, k, v, lc)


def _suffix_sum_matrix():
    j = np.arange(TQ)[:, None]
    s = np.arange(TQ)[None, :]
    return (j >= s).astype(np.float32)


def _stickbreak_kernel(q_ref, k_ref, v_ref, u_ref, o_ref, acc_ref, run_ref):
    qb = pl.program_id(1)
    q = q_ref[0]
    umat = u_ref[...]
    acc_ref[...] = jnp.zeros(acc_ref.shape, F32)
    run_ref[...] = jnp.zeros(run_ref.shape, F32)
    lane_head = lax.broadcasted_iota(jnp.int32, (TQ, LANES), 1) >> 6
    qstack = []
    for pair in range(2):
        qp = q[:, pair * LANES:(pair + 1) * LANES]
        qstack.append(jnp.concatenate(
            [jnp.where(lane_head == hh, qp, jnp.zeros_like(qp)) for hh in range(2)], axis=0))
    nh = HEADS_PER_MIXER
    row = lax.broadcasted_iota(jnp.int32, (nh * TQ, TQ), 0) & (TQ - 1)
    colk = lax.broadcasted_iota(jnp.int32, (nh * TQ, TQ), 1)
    causal = colk < row

    def tile(kb, diagonal):
        koff = pl.multiple_of(kb * TQ, TQ)
        kt = k_ref[0, pl.ds(koff, TQ), :]
        vt = v_ref[0, pl.ds(koff, TQ), :]
        y = jnp.concatenate(
            [lax.dot_general(qstack[pair], kt[:, pair * LANES:(pair + 1) * LANES], _NT, preferred_element_type=F32)
             for pair in range(2)], axis=0) * LOG2E
        nl = jnp.maximum(y, 0.0) + jnp.log2(1.0 + jnp.exp2(-jnp.abs(y)))
        if diagonal:
            nl = jnp.where(causal, nl, 0.0)
        cs = jnp.dot(nl.astype(BF16), umat, preferred_element_type=F32)
        run = run_ref[...]
        csum = cs + run
        a = jnp.exp2(y - csum)
        if diagonal:
            a = jnp.where(causal, a, 0.0)
        a = a.astype(BF16)
        for pair in range(2):
            cols = slice(pair * LANES, (pair + 1) * LANES)
            pv = jnp.dot(a[2 * pair * TQ:(2 * pair + 2) * TQ, :], vt[:, cols], preferred_element_type=F32)
            acc_ref[:, cols] += jnp.where(lane_head == 0, pv[:TQ, :], pv[TQ:, :])
        run_ref[...] = run + cs[:, 0:1]

    tile(qb, True)

    def body(j, carry):
        tile(qb - 1 - j, False)
        return carry

    lax.fori_loop(0, qb, body, 0)
    o_ref[0] = acc_ref[...]


def _stickbreak(q, k, v, umat):
    b, s, w = q.shape
    return pl.pallas_call(
        _stickbreak_kernel,
        grid=(b, s // TQ),
        in_specs=[pl.BlockSpec((1, TQ, w), lambda bi, qi: (bi, qi, 0)),
                  pl.BlockSpec((1, s, w), lambda bi, qi: (bi, 0, 0)),
                  pl.BlockSpec((1, s, w), lambda bi, qi: (bi, 0, 0)),
                  pl.BlockSpec(umat.shape, lambda bi, qi: (0, 0))],
        out_specs=pl.BlockSpec((1, TQ, w), lambda bi, qi: (bi, qi, 0)),
        out_shape=jax.ShapeDtypeStruct((b, s, w), F32),
        scratch_shapes=[pltpu.VMEM((TQ, w), F32), pltpu.VMEM((HEADS_PER_MIXER * TQ, 1), F32)],
        compiler_params=_params("arbitrary", "arbitrary"),
    )(q, k, v, umat)


def _mixer_out_kernel(x_ref, oa_ref, ob_ref, oc_ref, od_ref, gho_ref, wout_ref, gffn_ref, bd_ref,
                      x1_ref, h2_ref):
    bd = bd_ref[...]
    acc = x_ref[...]
    for p, r in enumerate((oa_ref, ob_ref, oc_ref, od_ref)):
        cols = slice(p * MIX_W, (p + 1) * MIX_W)
        y = _head_rms(r[...], gho_ref[:, cols], bd).astype(BF16)
        acc = acc + jnp.dot(y, wout_ref[cols, :], preferred_element_type=F32)
    x1_ref[...] = acc
    h2_ref[...] = _row_rms(acc, gffn_ref[...]).astype(BF16)


def _mixer_out(x2d, oa, ob, oc, od, gho, wout, gffn, bd):
    t, d = x2d.shape
    full = lambda a: pl.BlockSpec(a.shape, lambda i: (0,) * a.ndim)
    tok = lambda w: pl.BlockSpec((TM, w), lambda i: (i, 0))
    return pl.pallas_call(
        _mixer_out_kernel,
        grid=(t // TM,),
        in_specs=[tok(d)] + [tok(MIX_W)] * 4 + [full(gho), full(wout), full(gffn), full(bd)],
        out_specs=[tok(d), tok(d)],
        out_shape=[jax.ShapeDtypeStruct((t, d), F32), jax.ShapeDtypeStruct((t, d), BF16)],
        compiler_params=_params("arbitrary"),
    )(x2d, oa, ob, oc, od, gho, wout, gffn, bd)


def _top16_rows(s):
    kidx = lax.broadcasted_iota(jnp.int32, s.shape, 0).astype(F32)
    rank = jnp.full(s.shape, float(PEER_TOPK), F32)
    vals = []
    for r in range(PEER_TOPK):
        m = jnp.max(s, axis=0, keepdims=True)
        first = jnp.min(jnp.where(s == m, kidx, float(PEER_N_KEYS)), axis=0, keepdims=True)
        hit = kidx == first
        rank = jnp.where(hit, float(r), rank)
        s = jnp.where(hit, -jnp.inf, s)
        vals.append(m)
    return vals, rank


def _top16_rows_unique(s):
    rank = jnp.full(s.shape, float(PEER_TOPK), F32)
    vals = []
    for r in range(PEER_TOPK):
        m = jnp.max(s, axis=0, keepdims=True)
        hit = s == m
        rank = jnp.where(hit, float(r), rank)
        s = jnp.where(hit, -jnp.inf, s)
        vals.append(m)
    hits = jnp.sum(jnp.where(rank < float(PEER_TOPK), 1.0, 0.0), axis=0, keepdims=True)
    return vals, rank, hits


def _peer_route_kernel(h2_ref, wqt_ref, keys_ref, r2_ref, e2_ref, n_ref, e1_ref, st_ref):
    tm = h2_ref.shape[0]
    qt = lax.dot_general(wqt_ref[...], h2_ref[...], _NT, preferred_element_type=F32).astype(BF16)
    for hp in range(2 * PEER_HEADS):
        rows = slice(hp * PEER_N_KEYS, (hp + 1) * PEER_N_KEYS)
        st_ref[hp] = jnp.dot(keys_ref[hp], qt[rows, :], preferred_element_type=F32)

    nlb = tm // LANES
    sub = lax.broadcasted_iota(jnp.int32, (8, LANES), 0).astype(F32)
    pos = jnp.concatenate([sub, sub + 8.0] + [sub + 16.0 * a for a in range(1, 8)] + [(sub + 8.0) * 16.0], axis=0)
    sub16 = lax.broadcasted_iota(jnp.int32, (PEER_TOPK, LANES), 0).astype(F32)

    def candidates(v1, v2):
        sv1 = jnp.zeros((PEER_TOPK, LANES), F32)
        sv2 = jnp.zeros((PEER_TOPK, LANES), F32)
        for r in range(PEER_TOPK):
            sv1 = jnp.where(sub16 == float(r), v1[r], sv1)
            sv2 = jnp.where(sub16 == float(r), v2[r], sv2)
        return jnp.concatenate(
            [v1[0] + sv2[0:8], v1[0] + sv2[8:16]] + [v1[a] + sv2[0:8] for a in range(1, 8)] + [sv1[8:16] + v2[0]],
            axis=0)

    def store(hd, lb, s1, s2, v1, v2, rk2, cnt, zsum):
        r2_ref[hd, lb] = rk2.astype(BF16)
        e2_ref[hd, lb] = jnp.exp(s2 - v2[0]).astype(BF16)
        n_ref[hd, lb] = cnt
        e1_ref[hd, lb] = jnp.exp(s1 - v1[0]) * (1.0 / zsum)

    def fast(hd, lb):
        lo = pl.multiple_of(lb * LANES, LANES)
        s1 = st_ref[2 * hd, :, pl.ds(lo, LANES)]
        s2 = st_ref[2 * hd + 1, :, pl.ds(lo, LANES)]

        v1, rk1, hits1 = _top16_rows_unique(s1)
        v2, rk2, hits2 = _top16_rows_unique(s2)
        cand = candidates(v1, v2)
        sel = jnp.zeros(cand.shape, F32)
        zsum = jnp.zeros((1, LANES), F32)
        top0 = v1[0] + v2[0]
        for r in range(PEER_TOPK):
            m = jnp.max(cand, axis=0, keepdims=True)
            hit = cand == m
            cand = jnp.where(hit, -jnp.inf, cand)
            sel = jnp.where(hit, 1.0, sel)
            zsum = zsum + jnp.exp(m - top0)
        na = [jnp.sum(sel[0:16], axis=0, keepdims=True)]
        na += [jnp.sum(sel[8 + 8 * a:16 + 8 * a], axis=0, keepdims=True) for a in range(1, 8)]
        last = sel[72:80]
        cnt = jnp.zeros((PEER_N_KEYS, LANES), F32)
        for a in range(8):
            cnt = jnp.where(rk1 == float(a), na[a], cnt)
        for a in range(8, PEER_TOPK):
            cnt = jnp.where(rk1 == float(a), last[a - 8:a - 7], cnt)
        store(hd, lb, s1, s2, v1, v2, rk2, cnt, zsum)
        picked = jnp.sum(sel, axis=0, keepdims=True)
        return jnp.max(jnp.maximum(jnp.maximum(hits1, hits2), picked))

    def exact(hd, lb):
        lo = pl.multiple_of(lb * LANES, LANES)
        s1 = st_ref[2 * hd, :, pl.ds(lo, LANES)]
        s2 = st_ref[2 * hd + 1, :, pl.ds(lo, LANES)]
        v1, rk1 = _top16_rows(s1)
        v2, rk2 = _top16_rows(s2)
        cand = candidates(v1, v2)
        cnt = jnp.zeros((PEER_N_KEYS, LANES), F32)
        zsum = jnp.zeros((1, LANES), F32)
        top0 = v1[0] + v2[0]
        for r in range(PEER_TOPK):
            m = jnp.max(cand, axis=0, keepdims=True)
            pr = jnp.min(jnp.where(cand == m, pos, 4096.0), axis=0, keepdims=True)
            cand = jnp.where(pos == pr, -jnp.inf, cand)
            cnt = cnt + jnp.where(rk1 == jnp.floor(pr * (1.0 / 16.0)), 1.0, 0.0)
            zsum = zsum + jnp.exp(m - top0)
        store(hd, lb, s1, s2, v1, v2, rk2, cnt, zsum)

    def body(it, carry):
        per = nlb // ROUTE_BLOCKS_PER_TRIP
        hd = it // per
        lb0 = ROUTE_BLOCKS_PER_TRIP * (it % per)
        worst = [fast(hd, lb0 + k) for k in range(ROUTE_BLOCKS_PER_TRIP)]
        for k in range(ROUTE_BLOCKS_PER_TRIP):
            pl.when(worst[k] > PEER_TOPK + 0.5)(functools.partial(exact, hd, lb0 + k))
        return carry

    lax.fori_loop(0, PEER_HEADS * nlb // ROUTE_BLOCKS_PER_TRIP, body, 0)


def _peer_route(h2, wqt, keys):
    t, d = h2.shape
    tabs = lambda dt: jax.ShapeDtypeStruct((PEER_HEADS, t // LANES, PEER_N_KEYS, LANES), dt)
    tab_spec = pl.BlockSpec((PEER_HEADS, TM // LANES, PEER_N_KEYS, LANES), lambda i: (0, i, 0, 0))
    return pl.pallas_call(
        _peer_route_kernel,
        grid=(t // TM,),
        in_specs=[pl.BlockSpec((TM, d), lambda i: (i, 0)),
                  pl.BlockSpec(wqt.shape, lambda i: (0, 0)),
                  pl.BlockSpec(keys.shape, lambda i: (0, 0, 0))],
        out_specs=[tab_spec] * 4,
        out_shape=[tabs(BF16), tabs(BF16), tabs(F32), tabs(F32)],
        scratch_shapes=[pltpu.VMEM((2 * PEER_HEADS, PEER_N_KEYS, TM), F32)],
        compiler_params=_params("arbitrary"),
    )(h2, wqt, keys)


def _peer_ffn_kernel(h2_ref, u_ref, vt_ref, r2_ref, e2_ref, n_ref, e1_ref, x1_ref, o_ref,
                     acc_ref, at_ref, wt_ref):
    c = pl.program_id(1)
    tm = h2_ref.shape[0]
    ec = u_ref.shape[0]
    ngroup = ec // (PEER_GROUP * PEER_N_KEYS)

    @pl.when(c == 0)
    def _():
        acc_ref[...] = jnp.zeros(acc_ref.shape, F32)

    at_ref[...] = lax.dot_general(u_ref[...], h2_ref[...], _NT, preferred_element_type=F32)

    def group(gi, carry):
        grp = pl.ds(pl.multiple_of((c * ngroup + gi) * PEER_GROUP, PEER_GROUP), PEER_GROUP)
        for k in range(PEER_GROUP):
            rows = pl.ds(pl.multiple_of((gi * PEER_GROUP + k) * PEER_N_KEYS, PEER_N_KEYS), PEER_N_KEYS)
            for lb in range(tm // LANES):
                cols = slice(lb * LANES, (lb + 1) * LANES)
                gw = jnp.zeros((PEER_N_KEYS, LANES), BF16)
                for hd in range(PEER_HEADS):
                    nrow = n_ref[hd, lb, grp, :].astype(BF16)[k:k + 1, :]
                    e1row = e1_ref[hd, lb, grp, :].astype(BF16)[k:k + 1, :]
                    margin = nrow - r2_ref[hd, lb]
                    gw = gw + jnp.maximum(jnp.minimum(e1row * e2_ref[hd, lb], margin), 0.0)
                wt_ref[rows, cols] = jax.nn.gelu(at_ref[rows, cols]).astype(BF16) * gw
        return carry

    lax.fori_loop(0, ngroup, group, 0)
    acc_ref[...] += jnp.dot(vt_ref[...], wt_ref[...], preferred_element_type=F32)

    @pl.when(c == pl.num_programs(1) - 1)
    def _():
        o_ref[...] = x1_ref[...] + acc_ref[...].T


def _peer_ffn(h2, u, vt, r2, e2, n, e1, x1):
    t, d = h2.shape
    ne = u.shape[0]
    tab_spec = pl.BlockSpec((PEER_HEADS, TM // LANES, PEER_N_KEYS, LANES), lambda i, c: (0, i, 0, 0))
    return pl.pallas_call(
        _peer_ffn_kernel,
        grid=(t // TM, ne // PEER_EC),
        in_specs=[pl.BlockSpec((TM, d), lambda i, c: (i, 0)),
                  pl.BlockSpec((PEER_EC, d), lambda i, c: (c, 0)),
                  pl.BlockSpec((d, PEER_EC), lambda i, c: (0, c)),
                  tab_spec, tab_spec, tab_spec, tab_spec,
                  pl.BlockSpec((TM, d), lambda i, c: (i, 0))],
        out_specs=pl.BlockSpec((TM, d), lambda i, c: (i, 0)),
        out_shape=jax.ShapeDtypeStruct((t, d), F32),
        scratch_shapes=[pltpu.VMEM((d, TM), F32), pltpu.VMEM((PEER_EC, TM), F32), pltpu.VMEM((PEER_EC, TM), BF16)],
        compiler_params=_params("arbitrary", "arbitrary"),
    )(h2, u, vt, r2, e2, n, e1, x1)


def _tile_heads(g):
    return jnp.tile(g, HEADS_PER_MIXER)[None, :]


def kernel(x, g_mix_norm, w_in, w_sgu, b_sgu, g_sgu_v, g_q_dil, g_k_dil, conv_w, g_q_sb, g_k_sb, g_head_out,
           w_out, g_ffn_norm, w_peer_q, peer_sub_keys, peer_u, peer_v):
    b, s, d = x.shape
    depth = w_in.shape[0]
    assert d == 4 * MIX_W and s % TM == 0 and s >= DIL_WIN
    assert w_in.shape[2] == N_IN_PIECES * MIX_W

    hh = np.arange(MIX_W) // HEAD_DIM
    bd = jnp.asarray(hh[:, None] == hh[None, :], BF16)
    lc = jnp.asarray(_dilated_logcount_table())
    umat = jnp.asarray(_suffix_sum_matrix(), BF16)

    x2d = x.reshape(b * s, d)
    for l in range(depth):
        wsgu_cat = w_sgu[l].transpose(1, 0, 2).reshape(SGU_CHUNK, HEADS_PER_MIXER * SGU_CHUNK)
        bsgu_tile = jnp.repeat(b_sgu[l].T, HEAD_DIM, axis=1)
        oa, oc, dq, dk, dv, bq, bk, bv = _mixer_in(
            x2d, g_mix_norm[l][None, :], w_in[l].astype(BF16), wsgu_cat, bsgu_tile, g_sgu_v[l][None, :],
            _tile_heads(g_q_dil[l]), _tile_heads(g_k_dil[l]), conv_w[l], _tile_heads(g_q_sb[l]),
            _tile_heads(g_k_sb[l]), bd, seq=s)
        r3 = lambda a: a.reshape(b, s, MIX_W)
        ob = _dilated(r3(dq), r3(dk), r3(dv), lc).reshape(b * s, MIX_W)
        od = _stickbreak(r3(bq), r3(bk), r3(bv), umat).reshape(b * s, MIX_W)
        x1, h2 = _mixer_out(x2d, oa, ob, oc, od, g_head_out[l][None, :], w_out[l].astype(BF16),
                            g_ffn_norm[l][None, :], bd)
        wqt = w_peer_q[l].T.astype(BF16)
        keys = peer_sub_keys[l].reshape(2 * PEER_HEADS, PEER_N_KEYS, -1).astype(BF16)
        r2, e2, n, e1 = _peer_route(h2, wqt, keys)
        x2d = _peer_ffn(h2, peer_u[l].astype(BF16), peer_v[l].T.astype(BF16), r2, e2, n, e1, x1)
    return x2d.reshape(b, s, d)
```

```python
import functools

import numpy as np
import jax
import jax.numpy as jnp
from jax import lax
from jax.experimental import pallas as pl
from jax.experimental.pallas import tpu as pltpu

HEAD_DIM = 64
MIX_W = 256
HEADS_PER_MIXER = MIX_W // HEAD_DIM
N_IN_PIECES = 11
SGU_CHUNK = 128
RMS_EPS = 1e-6
NEG = -1e30
LOG2E = 1.4426950408889634
PEER_HEADS = 8
PEER_N_KEYS = 128
PEER_TOPK = 16
ALIBI_SLOPES = tuple(2.0 ** (-8.0 * (i + 1) / HEADS_PER_MIXER) for i in range(HEADS_PER_MIXER))

LANES = 128
TM = 512
TQ = 256
DIL_MAX_DIST = 2048
DIL_NWIN = DIL_MAX_DIST // TQ + 1
DIL_WIN = DIL_NWIN * TQ
PEER_EC = 2048
ROUTE_BLOCKS_PER_TRIP = 4
PEER_GROUP = 8
VMEM_LIMIT = 56 * 1024 * 1024

F32 = jnp.float32
BF16 = jnp.bfloat16
_NT = (((1,), (1,)), ((), ()))


def _params(*sem, flags=None):
    return pltpu.CompilerParams(dimension_semantics=sem, vmem_limit_bytes=VMEM_LIMIT, flags=flags)


def _head_sumsq(x2, bd):
    hi = x2.astype(BF16)
    lo = (x2 - hi.astype(F32)).astype(BF16)
    return (jnp.dot(hi, bd, preferred_element_type=F32) + jnp.dot(lo, bd, preferred_element_type=F32))


def _head_rms(x, g, bd):
    ss = _head_sumsq(x * x, bd)
    return x * lax.rsqrt(ss * (1.0 / HEAD_DIM) + RMS_EPS) * g


def _gelu_tanh(x):
    k = 2.0 * 0.7978845608028654 * LOG2E
    t = x * x * (-0.044715 * k) - k
    return x / (1.0 + jnp.exp2(x * t))


def _row_rms(x, g):
    ms = jnp.mean(x * x, axis=-1, keepdims=True)
    return x * lax.rsqrt(ms + RMS_EPS) * g


def _mixer_in_kernel(x_ref, gmix_ref, win_ref, wsgu_ref, bsgu_ref, gv_ref, gqd_ref, gkd_ref, convw_ref,
                     gqs_ref, gks_ref, bd_ref,
                     oa_ref, oc_ref, dq_ref, dk_ref, dv_ref, bq_ref, bk_ref, bv_ref,
                     ybuf_ref, *, tiles_per_seq):
    i = pl.program_id(0)
    tm = x_ref.shape[0]
    h = _row_rms(x_ref[...], gmix_ref[...]).astype(BF16)
    bd = bd_ref[...]
    scale = HEAD_DIM ** -0.5

    def piece(p):
        return jnp.dot(h, win_ref[:, p * MIX_W:(p + 1) * MIX_W], preferred_element_type=F32)

    u = jax.nn.gelu(piece(0))
    v = _head_rms(jax.nn.gelu(piece(1)), gv_ref[...], bd)
    t_idx = lax.broadcasted_iota(jnp.int32, (SGU_CHUNK, HEADS_PER_MIXER * SGU_CHUNK), 0)
    s_idx = lax.broadcasted_iota(jnp.int32, (SGU_CHUNK, HEADS_PER_MIXER * SGU_CHUNK), 1) & (SGU_CHUNK - 1)
    wcat = jnp.where(s_idx <= t_idx, wsgu_ref[...], 0.0).astype(BF16)
    lane_head = lax.broadcasted_iota(jnp.int32, (SGU_CHUNK, MIX_W), 1) >> 6
    bias = bsgu_ref[...]
    for c in range(tm // SGU_CHUNK):
        rows = slice(c * SGU_CHUNK, (c + 1) * SGU_CHUNK)
        vc = v[rows, :]
        vstack = jnp.concatenate(
            [jnp.where(lane_head == hh, vc, 0.0).astype(BF16) for hh in range(HEADS_PER_MIXER)], axis=0)
        mixed = jnp.dot(wcat, vstack, preferred_element_type=F32) + bias
        oa_ref[rows, :] = u[rows, :] * mixed

    dq_ref[...] = (_head_rms(piece(2), gqd_ref[...], bd) * scale).astype(BF16)
    dk_ref[...] = _head_rms(piece(3), gkd_ref[...], bd).astype(BF16)
    dv_ref[...] = piece(4).astype(BF16)
    bq_ref[...] = (_head_rms(piece(8), gqs_ref[...], bd) * scale).astype(BF16)
    bk_ref[...] = _head_rms(piece(9), gks_ref[...], bd).astype(BF16)
    bv_ref[...] = piece(10).astype(BF16)

    cb = piece(5)
    y = piece(6) * piece(7)

    @pl.when(i % tiles_per_seq == 0)
    def _():
        ybuf_ref[0:8, :] = jnp.zeros((8, MIX_W), F32)

    ybuf_ref[8:8 + tm, :] = y
    y1 = ybuf_ref[7:7 + tm, :]
    y2 = ybuf_ref[6:6 + tm, :]
    w = convw_ref[...]
    oc_ref[...] = cb * (w[0:1, :] * y2 + w[1:2, :] * y1 + w[2:3, :] * y)
    ybuf_ref[0:8, :] = y[tm - 8:tm, :]


def _mixer_in(x2d, gmix, win, wsgu_cat, bsgu_tile, gv, gqd, gkd, convw, gqs, gks, bd, *, seq):
    t = x2d.shape[0]
    d = x2d.shape[1]
    full = lambda a: pl.BlockSpec(a.shape, lambda i: (0,) * a.ndim)
    tok = lambda w: pl.BlockSpec((TM, w), lambda i: (i, 0))
    consts = (gmix, win, wsgu_cat, bsgu_tile, gv, gqd, gkd, convw, gqs, gks, bd)
    out_shape = ([jax.ShapeDtypeStruct((t, MIX_W), F32)] * 2 + [jax.ShapeDtypeStruct((t, MIX_W), BF16)] * 6)
    return pl.pallas_call(
        functools.partial(_mixer_in_kernel, tiles_per_seq=seq // TM),
        grid=(t // TM,),
        in_specs=[tok(d)] + [full(a) for a in consts],
        out_specs=[tok(MIX_W)] * 8,
        out_shape=out_shape,
        scratch_shapes=[pltpu.VMEM((TM + 8, MIX_W), F32)],
        compiler_params=_params("arbitrary"),
    )(x2d, *consts)


def _dilated_logcount_table():
    n = 2 * DIL_NWIN - 1
    qi = np.arange(TQ)[:, None]
    kj = np.arange(TQ)[None, :]
    tiles = []
    for u in range(n):
        off = DIL_NWIN - 1 - u
        dd = off * TQ + qi - kj
        cnt = ((dd >= 0) & (dd <= 128)).astype(np.int64)
        cnt += ((dd >= 0) & (dd <= 512) & (dd % 4 == 0))
        cnt += ((dd >= 0) & (dd <= 2048) & (dd % 16 == 0))
        tiles.append(np.where(cnt > 0, np.log(np.maximum(cnt, 1).astype(np.float64)), NEG))
    return np.concatenate(tiles, axis=1).astype(np.float32)


def _dilated_kernel(q_ref, k_ref, v_ref, lc_ref, o_ref):
    qb = pl.program_id(1)
    o0 = jnp.minimum(qb, DIL_NWIN - 1)
    kstart = pl.multiple_of((qb - o0) * TQ, TQ)
    cstart = pl.multiple_of((DIL_NWIN - 1 - o0) * TQ, TQ)
    lcw = lc_ref[:, pl.ds(cstart, DIL_WIN)]
    qi = lax.broadcasted_iota(jnp.int32, (TQ, DIL_WIN), 0)
    col = lax.broadcasted_iota(jnp.int32, (TQ, DIL_WIN), 1)
    dist = (o0 * TQ + qi - col).astype(F32)
    q = q_ref[0]
    kw = k_ref[0, pl.ds(kstart, DIL_WIN), :]
    vw = v_ref[0, pl.ds(kstart, DIL_WIN), :]
    lane_head = lax.broadcasted_iota(jnp.int32, (TQ, LANES), 1) >> 6
    for pair in range(2):
        cols = slice(pair * LANES, (pair + 1) * LANES)
        qp, kp, vp = q[:, cols], kw[:, cols], vw[:, cols]
        outs = []
        for hh in range(2):
            qm = jnp.where(lane_head == hh, qp, jnp.zeros_like(qp))
            s = lax.dot_general(qm, kp, _NT, preferred_element_type=F32)
            s = s + (lcw - ALIBI_SLOPES[2 * pair + hh] * dist)
            m = jnp.max(s, axis=-1, keepdims=True)
            p = jnp.exp(s - m)
            l = jnp.sum(p, axis=-1, keepdims=True)
            o = jnp.dot(p.astype(BF16), vp, preferred_element_type=F32)
            outs.append(o / l)
        o_ref[0, :, cols] = jnp.where(lane_head == 0, outs[0], outs[1])


def _dilated(q, k, v, lc):
    b, s, w = q.shape
    return pl.pallas_call(
        _dilated_kernel,
        grid=(b, s // TQ),
        in_specs=[pl.BlockSpec((1, TQ, w), lambda bi, qi: (bi, qi, 0)),
                  pl.BlockSpec((1, s, w), lambda bi, qi: (bi, 0, 0)),
                  pl.BlockSpec((1, s, w), lambda bi, qi: (bi, 0, 0)),
                  pl.BlockSpec(lc.shape, lambda bi, qi: (0, 0))],
        out_specs=pl.BlockSpec((1, TQ, w), lambda bi, qi: (bi, qi, 0)),
        out_shape=jax.ShapeDtypeStruct((b, s, w), F32),
        compiler_params=_params("arbitrary", "arbitrary"),
    )(q, k, v, lc)


def _suffix_sum_matrix():
    j = np.arange(TQ)[:, None]
    s = np.arange(TQ)[None, :]
    return (j >= s).astype(np.float32)


def _stickbreak_kernel(q_ref, k_ref, v_ref, u_ref, o_ref, acc_ref, run_ref):
    qb = pl.program_id(1)
    q = q_ref[0]
    umat = u_ref[...]
    acc_ref[...] = jnp.zeros(acc_ref.shape, F32)
    run_ref[...] = jnp.zeros(run_ref.shape, F32)
    lane_head = lax.broadcasted_iota(jnp.int32, (TQ, LANES), 1) >> 6
    qstack = []
    for pair in range(2):
        qp = q[:, pair * LANES:(pair + 1) * LANES]
        qstack.append(jnp.concatenate(
            [jnp.where(lane_head == hh, qp, jnp.zeros_like(qp)) for hh in range(2)], axis=0))
    nh = HEADS_PER_MIXER
    row = lax.broadcasted_iota(jnp.int32, (nh * TQ, TQ), 0) & (TQ - 1)
    colk = lax.broadcasted_iota(jnp.int32, (nh * TQ, TQ), 1)
    causal = colk < row

    def tile(kb, diagonal):
        koff = pl.multiple_of(kb * TQ, TQ)
        kt = k_ref[0, pl.ds(koff, TQ), :]
        vt = v_ref[0, pl.ds(koff, TQ), :]
        y = jnp.concatenate(
            [lax.dot_general(qstack[pair], kt[:, pair * LANES:(pair + 1) * LANES], _NT, preferred_element_type=F32)
             for pair in range(2)], axis=0) * LOG2E
        nl = jnp.maximum(y, 0.0) + jnp.log2(1.0 + jnp.exp2(-jnp.abs(y)))
        if diagonal:
            nl = jnp.where(causal, nl, 0.0)
        cs = jnp.dot(nl.astype(BF16), umat, preferred_element_type=F32)
        run = run_ref[...]
        csum = cs + run
        a = jnp.exp2(y - csum)
        if diagonal:
            a = jnp.where(causal, a, 0.0)
        a = a.astype(BF16)
        for pair in range(2):
            cols = slice(pair * LANES, (pair + 1) * LANES)
            pv = jnp.dot(a[2 * pair * TQ:(2 * pair + 2) * TQ, :], vt[:, cols], preferred_element_type=F32)
            acc_ref[:, cols] += jnp.where(lane_head == 0, pv[:TQ, :], pv[TQ:, :])
        run_ref[...] = run + cs[:, 0:1]

    tile(qb, True)

    def body(j, carry):
        tile(qb - 1 - j, False)
        return carry

    lax.fori_loop(0, qb, body, 0)
    o_ref[0] = acc_ref[...]


def _stickbreak(q, k, v, umat):
    b, s, w = q.shape
    return pl.pallas_call(
        _stickbreak_kernel,
        grid=(b, s // TQ),
        in_specs=[pl.BlockSpec((1, TQ, w), lambda bi, qi: (bi, qi, 0)),
                  pl.BlockSpec((1, s, w), lambda bi, qi: (bi, 0, 0)),
                  pl.BlockSpec((1, s, w), lambda bi, qi: (bi, 0, 0)),
                  pl.BlockSpec(umat.shape, lambda bi, qi: (0, 0))],
        out_specs=pl.BlockSpec((1, TQ, w), lambda bi, qi: (bi, qi, 0)),
        out_shape=jax.ShapeDtypeStruct((b, s, w), F32),
        scratch_shapes=[pltpu.VMEM((TQ, w), F32), pltpu.VMEM((HEADS_PER_MIXER * TQ, 1), F32)],
        compiler_params=_params("arbitrary", "arbitrary"),
    )(q, k, v, umat)


def _mixer_out_kernel(x_ref, oa_ref, ob_ref, oc_ref, od_ref, gho_ref, wout_ref, gffn_ref, bd_ref,
                      x1_ref, h2_ref):
    bd = bd_ref[...]
    acc = x_ref[...]
    for p, r in enumerate((oa_ref, ob_ref, oc_ref, od_ref)):
        cols = slice(p * MIX_W, (p + 1) * MIX_W)
        y = _head_rms(r[...], gho_ref[:, cols], bd).astype(BF16)
        acc = acc + jnp.dot(y, wout_ref[cols, :], preferred_element_type=F32)
    x1_ref[...] = acc
    h2_ref[...] = _row_rms(acc, gffn_ref[...]).astype(BF16)


def _mixer_out(x2d, oa, ob, oc, od, gho, wout, gffn, bd):
    t, d = x2d.shape
    full = lambda a: pl.BlockSpec(a.shape, lambda i: (0,) * a.ndim)
    tok = lambda w: pl.BlockSpec((TM, w), lambda i: (i, 0))
    return pl.pallas_call(
        _mixer_out_kernel,
        grid=(t // TM,),
        in_specs=[tok(d)] + [tok(MIX_W)] * 4 + [full(gho), full(wout), full(gffn), full(bd)],
        out_specs=[tok(d), tok(d)],
        out_shape=[jax.ShapeDtypeStruct((t, d), F32), jax.ShapeDtypeStruct((t, d), BF16)],
        compiler_params=_params("arbitrary"),
    )(x2d, oa, ob, oc, od, gho, wout, gffn, bd)


def _top16_rows(s):
    kidx = lax.broadcasted_iota(jnp.int32, s.shape, 0).astype(F32)
    rank = jnp.full(s.shape, float(PEER_TOPK), F32)
    vals = []
    for r in range(PEER_TOPK):
        m = jnp.max(s, axis=0, keepdims=True)
        first = jnp.min(jnp.where(s == m, kidx, float(PEER_N_KEYS)), axis=0, keepdims=True)
        hit = kidx == first
        rank = jnp.where(hit, float(r), rank)
        s = jnp.where(hit, -jnp.inf, s)
        vals.append(m)
    return vals, rank


def _top16_rows_unique(s, want_rank):
    rank = jnp.full(s.shape, float(PEER_TOPK), F32) if want_rank else None
    vals = []
    for r in range(PEER_TOPK):
        m = jnp.max(s, axis=0, keepdims=True)
        hit = s == m
        if want_rank:
            rank = jnp.where(hit, float(r), rank)
        s = jnp.where(hit, -jnp.inf, s)
        vals.append(m)
    hits = jnp.sum(jnp.where(s == -jnp.inf, 1.0, 0.0), axis=0, keepdims=True)
    return vals, rank, hits


def _peer_route_kernel(h2_ref, wqt_ref, keys_ref, r2_ref, e2_ref, n_ref, e1_ref, st_ref):
    tm = h2_ref.shape[0]
    qt = lax.dot_general(wqt_ref[...], h2_ref[...], _NT, preferred_element_type=F32).astype(BF16)
    for hp in range(2 * PEER_HEADS):
        rows = slice(hp * PEER_N_KEYS, (hp + 1) * PEER_N_KEYS)
        st_ref[hp] = jnp.dot(keys_ref[hp], qt[rows, :], preferred_element_type=F32)

    nlb = tm // LANES
    sub = lax.broadcasted_iota(jnp.int32, (8, LANES), 0).astype(F32)
    pos = jnp.concatenate([sub, sub + 8.0] + [sub + 16.0 * a for a in range(1, 8)] + [(sub + 8.0) * 16.0], axis=0)
    sub16 = lax.broadcasted_iota(jnp.int32, (PEER_TOPK, LANES), 0).astype(F32)

    def candidates(v1, v2):
        sv1 = jnp.zeros((PEER_TOPK, LANES), F32)
        sv2 = jnp.zeros((PEER_TOPK, LANES), F32)
        for r in range(PEER_TOPK):
            sv1 = jnp.where(sub16 == float(r), v1[r], sv1)
            sv2 = jnp.where(sub16 == float(r), v2[r], sv2)
        return jnp.concatenate(
            [v1[0] + sv2[0:8], v1[0] + sv2[8:16]] + [v1[a] + sv2[0:8] for a in range(1, 8)] + [sv1[8:16] + v2[0]],
            axis=0)

    def store(hd, lb, s1, s2, v1, v2, rk2, cnt, zsum):
        r2_ref[hd, lb] = rk2.astype(BF16)
        e2_ref[hd, lb] = jnp.exp(s2 - v2[0]).astype(BF16)
        n_ref[hd, lb] = cnt
        e1_ref[hd, lb] = jnp.exp(s1 - v1[0]) * (1.0 / zsum)

    def fast(hd, lb):
        lo = pl.multiple_of(lb * LANES, LANES)
        s1 = st_ref[2 * hd, :, pl.ds(lo, LANES)]
        s2 = st_ref[2 * hd + 1, :, pl.ds(lo, LANES)]

        v1, _, hits1 = _top16_rows_unique(s1, False)
        v2, rk2, hits2 = _top16_rows_unique(s2, True)
        cand = candidates(v1, v2)
        zsum = jnp.zeros((1, LANES), F32)
        top0 = v1[0] + v2[0]
        for r in range(PEER_TOPK):
            m = jnp.max(cand, axis=0, keepdims=True)
            cand = jnp.where(cand == m, -jnp.inf, cand)
            zsum = zsum + jnp.exp(m - top0)
        sel = jnp.where(cand == -jnp.inf, 1.0, 0.0)
        na = [jnp.sum(sel[0:16], axis=0, keepdims=True)]
        na += [jnp.sum(sel[8 + 8 * a:16 + 8 * a], axis=0, keepdims=True) for a in range(1, 8)]
        last = sel[72:80]
        cnt = jnp.zeros((PEER_N_KEYS, LANES), F32)
        for a in range(8):
            cnt = jnp.where(s1 == v1[a], na[a], cnt)
        for a in range(8, PEER_TOPK):
            cnt = jnp.where(s1 == v1[a], last[a - 8:a - 7], cnt)
        store(hd, lb, s1, s2, v1, v2, rk2, cnt, zsum)
        picked = jnp.sum(sel, axis=0, keepdims=True)
        return jnp.max(jnp.maximum(jnp.maximum(hits1, hits2), picked))

    def exact(hd, lb):
        lo = pl.multiple_of(lb * LANES, LANES)
        s1 = st_ref[2 * hd, :, pl.ds(lo, LANES)]
        s2 = st_ref[2 * hd + 1, :, pl.ds(lo, LANES)]
        v1, rk1 = _top16_rows(s1)
        v2, rk2 = _top16_rows(s2)
        cand = candidates(v1, v2)
        cnt = jnp.zeros((PEER_N_KEYS, LANES), F32)
        zsum = jnp.zeros((1, LANES), F32)
        top0 = v1[0] + v2[0]
        for r in range(PEER_TOPK):
            m = jnp.max(cand, axis=0, keepdims=True)
            pr = jnp.min(jnp.where(cand == m, pos, 4096.0), axis=0, keepdims=True)
            cand = jnp.where(pos == pr, -jnp.inf, cand)
            cnt = cnt + jnp.where(rk1 == jnp.floor(pr * (1.0 / 16.0)), 1.0, 0.0)
            zsum = zsum + jnp.exp(m - top0)
        store(hd, lb, s1, s2, v1, v2, rk2, cnt, zsum)

    def body(it, carry):
        per = nlb // ROUTE_BLOCKS_PER_TRIP
        hd = it // per
        lb0 = ROUTE_BLOCKS_PER_TRIP * (it % per)
        worst = [fast(hd, lb0 + k) for k in range(ROUTE_BLOCKS_PER_TRIP)]
        for k in range(ROUTE_BLOCKS_PER_TRIP):
            pl.when(worst[k] > PEER_TOPK + 0.5)(functools.partial(exact, hd, lb0 + k))
        return carry

    lax.fori_loop(0, PEER_HEADS * nlb // ROUTE_BLOCKS_PER_TRIP, body, 0)


def _peer_route(h2, wqt, keys):
    t, d = h2.shape
    tabs = lambda dt: jax.ShapeDtypeStruct((PEER_HEADS, t // LANES, PEER_N_KEYS, LANES), dt)
    tab_spec = pl.BlockSpec((PEER_HEADS, TM // LANES, PEER_N_KEYS, LANES), lambda i: (0, i, 0, 0))
    return pl.pallas_call(
        _peer_route_kernel,
        grid=(t // TM,),
        in_specs=[pl.BlockSpec((TM, d), lambda i: (i, 0)),
                  pl.BlockSpec(wqt.shape, lambda i: (0, 0)),
                  pl.BlockSpec(keys.shape, lambda i: (0, 0, 0))],
        out_specs=[tab_spec] * 4,
        out_shape=[tabs(BF16), tabs(BF16), tabs(F32), tabs(F32)],
        scratch_shapes=[pltpu.VMEM((2 * PEER_HEADS, PEER_N_KEYS, TM), F32)],
        compiler_params=_params("arbitrary"),
    )(h2, wqt, keys)


def _peer_ffn_kernel(h2_ref, u_ref, vt_ref, r2_ref, e2_ref, n_ref, e1_ref, x1_ref, o_ref,
                     acc_ref, at_ref, wt_ref):
    c = pl.program_id(1)
    tm = h2_ref.shape[0]
    ec = u_ref.shape[0]
    ngroup = ec // (PEER_GROUP * PEER_N_KEYS)

    @pl.when(c == 0)
    def _():
        acc_ref[...] = jnp.zeros(acc_ref.shape, F32)

    at_ref[...] = lax.dot_general(u_ref[...], h2_ref[...], _NT, preferred_element_type=F32).astype(BF16)

    def group(gi, carry):
        grp = pl.ds(pl.multiple_of((c * ngroup + gi) * PEER_GROUP, PEER_GROUP), PEER_GROUP)
        for k in range(PEER_GROUP):
            rows = pl.ds(pl.multiple_of((gi * PEER_GROUP + k) * PEER_N_KEYS, PEER_N_KEYS), PEER_N_KEYS)
            for lb in range(tm // LANES):
                cols = slice(lb * LANES, (lb + 1) * LANES)
                gw = jnp.zeros((PEER_N_KEYS, LANES), BF16)
                for hd in range(PEER_HEADS):
                    nrow = n_ref[hd, lb, grp, :].astype(BF16)[k:k + 1, :]
                    e1row = e1_ref[hd, lb, grp, :].astype(BF16)[k:k + 1, :]
                    margin = nrow - r2_ref[hd, lb]
                    gw = gw + jnp.maximum(jnp.minimum(e1row * e2_ref[hd, lb], margin), 0.0)
                wt_ref[rows, cols] = _gelu_tanh(at_ref[rows, cols]) * gw
        return carry

    lax.fori_loop(0, ngroup, group, 0)
    acc_ref[...] += jnp.dot(vt_ref[...], wt_ref[...], preferred_element_type=F32)

    @pl.when(c == pl.num_programs(1) - 1)
    def _():
        o_ref[...] = x1_ref[...] + acc_ref[...].T


def _peer_ffn(h2, u, vt, r2, e2, n, e1, x1):
    t, d = h2.shape
    ne = u.shape[0]
    tab_spec = pl.BlockSpec((PEER_HEADS, TM // LANES, PEER_N_KEYS, LANES), lambda i, c: (0, i, 0, 0))
    return pl.pallas_call(
        _peer_ffn_kernel,
        grid=(t // TM, ne // PEER_EC),
        in_specs=[pl.BlockSpec((TM, d), lambda i, c: (i, 0)),
                  pl.BlockSpec((PEER_EC, d), lambda i, c: (c, 0)),
                  pl.BlockSpec((d, PEER_EC), lambda i, c: (0, c)),
                  tab_spec, tab_spec, tab_spec, tab_spec,
                  pl.BlockSpec((TM, d), lambda i, c: (i, 0))],
        out_specs=pl.BlockSpec((TM, d), lambda i, c: (i, 0)),
        out_shape=jax.ShapeDtypeStruct((t, d), F32),
        scratch_shapes=[pltpu.VMEM((d, TM), F32), pltpu.VMEM((PEER_EC, TM), BF16), pltpu.VMEM((PEER_EC, TM), BF16)],
        compiler_params=_params("arbitrary", "arbitrary"),
    )(h2, u, vt, r2, e2, n, e1, x1)


def _tile_heads(g):
    return jnp.tile(g, HEADS_PER_MIXER)[None, :]


def kernel(x, g_mix_norm, w_in, w_sgu, b_sgu, g_sgu_v, g_q_dil, g_k_dil, conv_w, g_q_sb, g_k_sb, g_head_out,
           w_out, g_ffn_norm, w_peer_q, peer_sub_keys, peer_u, peer_v):
    b, s, d = x.shape
    depth = w_in.shape[0]
    assert d == 4 * MIX_W and s % TM == 0 and s >= DIL_WIN
    assert w_in.shape[2] == N_IN_PIECES * MIX_W

    hh = np.arange(MIX_W) // HEAD_DIM
    bd = jnp.asarray(hh[:, None] == hh[None, :], BF16)
    lc = jnp.asarray(_dilated_logcount_table())
    umat = jnp.asarray(_suffix_sum_matrix(), BF16)

    x2d = x.reshape(b * s, d)
    for l in range(depth):
        wsgu_cat = w_sgu[l].transpose(1, 0, 2).reshape(SGU_CHUNK, HEADS_PER_MIXER * SGU_CHUNK)
        bsgu_tile = jnp.repeat(b_sgu[l].T, HEAD_DIM, axis=1)
        oa, oc, dq, dk, dv, bq, bk, bv = _mixer_in(
            x2d, g_mix_norm[l][None, :], w_in[l].astype(BF16), wsgu_cat, bsgu_tile, g_sgu_v[l][None, :],
            _tile_heads(g_q_dil[l]), _tile_heads(g_k_dil[l]), conv_w[l], _tile_heads(g_q_sb[l]),
            _tile_heads(g_k_sb[l]), bd, seq=s)
        r3 = lambda a: a.reshape(b, s, MIX_W)
        ob = _dilated(r3(dq), r3(dk), r3(dv), lc).reshape(b * s, MIX_W)
        od = _stickbreak(r3(bq), r3(bk), r3(bv), umat).reshape(b * s, MIX_W)
        x1, h2 = _mixer_out(x2d, oa, ob, oc, od, g_head_out[l][None, :], w_out[l].astype(BF16),
                            g_ffn_norm[l][None, :], bd)
        wqt = w_peer_q[l].T.astype(BF16)
        keys = peer_sub_keys[l].reshape(2 * PEER_HEADS, PEER_N_KEYS, -1).astype(BF16)
        r2, e2, n, e1 = _peer_route(h2, wqt, keys)
        x2d = _peer_ffn(h2, peer_u[l].astype(BF16), peer_v[l].T.astype(BF16), r2, e2, n, e1, x1)
    return x2d.reshape(b, s, d)
```

```python
import functools

import numpy as np
import jax
import jax.numpy as jnp
from jax import lax
from jax.experimental import pallas as pl
from jax.experimental.pallas import tpu as pltpu

HEAD_DIM = 64
MIX_W = 256
HEADS_PER_MIXER = MIX_W // HEAD_DIM
N_IN_PIECES = 11
SGU_CHUNK = 128
RMS_EPS = 1e-6
NEG = -1e30
LOG2E = 1.4426950408889634
PEER_HEADS = 8
PEER_N_KEYS = 128
PEER_TOPK = 16
ALIBI_SLOPES = tuple(2.0 ** (-8.0 * (i + 1) / HEADS_PER_MIXER) for i in range(HEADS_PER_MIXER))

LANES = 128
TM = 512
TQ = 256
DIL_MAX_DIST = 2048
DIL_NWIN = DIL_MAX_DIST // TQ + 1
DIL_WIN = DIL_NWIN * TQ
DIL_SHORT_TILES = 4
PEER_EC = 2048
ROUTE_BLOCKS_PER_TRIP = 4
PEER_GROUP = 8
VMEM_LIMIT = 56 * 1024 * 1024

F32 = jnp.float32
BF16 = jnp.bfloat16
_NT = (((1,), (1,)), ((), ()))


def _params(*sem, flags=None):
    return pltpu.CompilerParams(dimension_semantics=sem, vmem_limit_bytes=VMEM_LIMIT, flags=flags)


def _head_sumsq(x2, bd):
    hi = x2.astype(BF16)
    lo = (x2 - hi.astype(F32)).astype(BF16)
    return (jnp.dot(hi, bd, preferred_element_type=F32) + jnp.dot(lo, bd, preferred_element_type=F32))


def _head_rms(x, g, bd):
    ss = _head_sumsq(x * x, bd)
    return x * lax.rsqrt(ss * (1.0 / HEAD_DIM) + RMS_EPS) * g


def _gelu_tanh(x):
    k = 2.0 * 0.7978845608028654 * LOG2E
    t = x * x * (-0.044715 * k) - k
    return x / (1.0 + jnp.exp2(x * t))


def _row_rms(x, g):
    ms = jnp.mean(x * x, axis=-1, keepdims=True)
    return x * lax.rsqrt(ms + RMS_EPS) * g


def _mixer_in_kernel(x_ref, gmix_ref, win_ref, wsgu_ref, bsgu_ref, gv_ref, gqd_ref, gkd_ref, convw_ref,
                     gqs_ref, gks_ref, bd_ref,
                     oa_ref, oc_ref, dq_ref, dk_ref, dv_ref, bq_ref, bk_ref, bv_ref,
                     ybuf_ref, *, tiles_per_seq):
    i = pl.program_id(0)
    tm = x_ref.shape[0]
    h = _row_rms(x_ref[...], gmix_ref[...]).astype(BF16)
    bd = bd_ref[...]
    scale = HEAD_DIM ** -0.5

    def piece(p):
        return jnp.dot(h, win_ref[:, p * MIX_W:(p + 1) * MIX_W], preferred_element_type=F32)

    u = jax.nn.gelu(piece(0))
    v = _head_rms(jax.nn.gelu(piece(1)), gv_ref[...], bd)
    t_idx = lax.broadcasted_iota(jnp.int32, (SGU_CHUNK, HEADS_PER_MIXER * SGU_CHUNK), 0)
    s_idx = lax.broadcasted_iota(jnp.int32, (SGU_CHUNK, HEADS_PER_MIXER * SGU_CHUNK), 1) & (SGU_CHUNK - 1)
    wcat = jnp.where(s_idx <= t_idx, wsgu_ref[...], 0.0).astype(BF16)
    lane_head = lax.broadcasted_iota(jnp.int32, (SGU_CHUNK, MIX_W), 1) >> 6
    bias = bsgu_ref[...]
    for c in range(tm // SGU_CHUNK):
        rows = slice(c * SGU_CHUNK, (c + 1) * SGU_CHUNK)
        vc = v[rows, :]
        vstack = jnp.concatenate(
            [jnp.where(lane_head == hh, vc, 0.0).astype(BF16) for hh in range(HEADS_PER_MIXER)], axis=0)
        mixed = jnp.dot(wcat, vstack, preferred_element_type=F32) + bias
        oa_ref[rows, :] = u[rows, :] * mixed

    dq_ref[...] = (_head_rms(piece(2), gqd_ref[...], bd) * scale).astype(BF16)
    dk_ref[...] = _head_rms(piece(3), gkd_ref[...], bd).astype(BF16)
    dv_ref[...] = piece(4).astype(BF16)
    bq_ref[...] = (_head_rms(piece(8), gqs_ref[...], bd) * scale).astype(BF16)
    bk_ref[...] = _head_rms(piece(9), gks_ref[...], bd).astype(BF16)
    bv_ref[...] = piece(10).astype(BF16)

    cb = piece(5)
    y = piece(6) * piece(7)

    @pl.when(i % tiles_per_seq == 0)
    def _():
        ybuf_ref[0:8, :] = jnp.zeros((8, MIX_W), F32)

    ybuf_ref[8:8 + tm, :] = y
    y1 = ybuf_ref[7:7 + tm, :]
    y2 = ybuf_ref[6:6 + tm, :]
    w = convw_ref[...]
    oc_ref[...] = cb * (w[0:1, :] * y2 + w[1:2, :] * y1 + w[2:3, :] * y)
    ybuf_ref[0:8, :] = y[tm - 8:tm, :]


def _mixer_in(x2d, gmix, win, wsgu_cat, bsgu_tile, gv, gqd, gkd, convw, gqs, gks, bd, *, seq):
    t = x2d.shape[0]
    d = x2d.shape[1]
    full = lambda a: pl.BlockSpec(a.shape, lambda i: (0,) * a.ndim)
    tok = lambda w: pl.BlockSpec((TM, w), lambda i: (i, 0))
    consts = (gmix, win, wsgu_cat, bsgu_tile, gv, gqd, gkd, convw, gqs, gks, bd)
    out_shape = ([jax.ShapeDtypeStruct((t, MIX_W), F32)] * 2 + [jax.ShapeDtypeStruct((t, MIX_W), BF16)] * 6)
    return pl.pallas_call(
        functools.partial(_mixer_in_kernel, tiles_per_seq=seq // TM),
        grid=(t // TM,),
        in_specs=[tok(d)] + [full(a) for a in consts],
        out_specs=[tok(MIX_W)] * 8,
        out_shape=out_shape,
        scratch_shapes=[pltpu.VMEM((TM + 8, MIX_W), F32)],
        compiler_params=_params("arbitrary"),
    )(x2d, *consts)


def _dilated_logcount_table():
    n = 2 * DIL_NWIN - 1
    qi = np.arange(TQ)[:, None]
    kj = np.arange(TQ)[None, :]
    tiles = []
    for u in range(n):
        off = DIL_NWIN - 1 - u
        dd = off * TQ + qi - kj
        cnt = ((dd >= 0) & (dd <= 128)).astype(np.int64)
        cnt += ((dd >= 0) & (dd <= 512) & (dd % 4 == 0))
        cnt += ((dd >= 0) & (dd <= 2048) & (dd % 16 == 0))
        tiles.append(np.where(cnt > 0, np.log(np.maximum(cnt, 1).astype(np.float64)), NEG))
    return np.concatenate(tiles, axis=1).astype(np.float32)


def _dilated_kernel(q_ref, k_ref, v_ref, lc_ref, o_ref):
    qb = pl.program_id(1)
    o0 = jnp.minimum(qb, DIL_NWIN - 1)
    kstart = pl.multiple_of((qb - o0) * TQ, TQ)
    cstart = pl.multiple_of((DIL_NWIN - 1 - o0) * TQ, TQ)
    lane_head = lax.broadcasted_iota(jnp.int32, (TQ, LANES), 1) >> 6

    def attend(win):
        lcw = lc_ref[:, pl.ds(cstart, win)]
        qi = lax.broadcasted_iota(jnp.int32, (TQ, win), 0)
        col = lax.broadcasted_iota(jnp.int32, (TQ, win), 1)
        dist = (o0 * TQ + qi - col).astype(F32)
        q = q_ref[0]
        kw = k_ref[0, pl.ds(kstart, win), :]
        vw = v_ref[0, pl.ds(kstart, win), :]
        for pair in range(2):
            cols = slice(pair * LANES, (pair + 1) * LANES)
            qp, kp, vp = q[:, cols], kw[:, cols], vw[:, cols]
            outs = []
            for hh in range(2):
                qm = jnp.where(lane_head == hh, qp, jnp.zeros_like(qp))
                s = lax.dot_general(qm, kp, _NT, preferred_element_type=F32)
                s = s + (lcw - ALIBI_SLOPES[2 * pair + hh] * dist)
                m = jnp.max(s, axis=-1, keepdims=True)
                p = jnp.exp(s - m)
                l = jnp.sum(p, axis=-1, keepdims=True)
                o = jnp.dot(p.astype(BF16), vp, preferred_element_type=F32)
                outs.append(o / l)
            o_ref[0, :, cols] = jnp.where(lane_head == 0, outs[0], outs[1])

    @pl.when(qb < DIL_SHORT_TILES)
    def _():
        attend(DIL_SHORT_TILES * TQ)

    @pl.when(qb >= DIL_SHORT_TILES)
    def _():
        attend(DIL_WIN)


def _dilated(q, k, v, lc):
    b, s, w = q.shape
    return pl.pallas_call(
        _dilated_kernel,
        grid=(b, s // TQ),
        in_specs=[pl.BlockSpec((1, TQ, w), lambda bi, qi: (bi, qi, 0)),
                  pl.BlockSpec((1, s, w), lambda bi, qi: (bi, 0, 0)),
                  pl.BlockSpec((1, s, w), lambda bi, qi: (bi, 0, 0)),
                  pl.BlockSpec(lc.shape, lambda bi, qi: (0, 0))],
        out_specs=pl.BlockSpec((1, TQ, w), lambda bi, qi: (bi, qi, 0)),
        out_shape=jax.ShapeDtypeStruct((b, s, w), F32),
        compiler_params=_params("arbitrary", "arbitrary"),
    )(q, k, v, lc)


def _suffix_sum_matrix():
    j = np.arange(TQ)[:, None]
    s = np.arange(TQ)[None, :]
    return (j >= s).astype(np.float32)


def _stickbreak_kernel(q_ref, k_ref, v_ref, u_ref, o_ref, acc_ref, run_ref):
    qb = pl.program_id(1)
    q = q_ref[0]
    umat = u_ref[...]
    acc_ref[...] = jnp.zeros(acc_ref.shape, F32)
    run_ref[...] = jnp.zeros(run_ref.shape, F32)
    lane_head = lax.broadcasted_iota(jnp.int32, (TQ, LANES), 1) >> 6
    qstack = []
    for pair in range(2):
        qp = q[:, pair * LANES:(pair + 1) * LANES]
        qstack.append(jnp.concatenate(
            [jnp.where(lane_head == hh, qp, jnp.zeros_like(qp)) for hh in range(2)], axis=0))
    nh = HEADS_PER_MIXER
    row = lax.broadcasted_iota(jnp.int32, (nh * TQ, TQ), 0) & (TQ - 1)
    colk = lax.broadcasted_iota(jnp.int32, (nh * TQ, TQ), 1)
    causal = colk < row

    def tile(kb, diagonal):
        koff = pl.multiple_of(kb * TQ, TQ)
        kt = k_ref[0, pl.ds(koff, TQ), :]
        vt = v_ref[0, pl.ds(koff, TQ), :]
        y = jnp.concatenate(
            [lax.dot_general(qstack[pair], kt[:, pair * LANES:(pair + 1) * LANES], _NT, preferred_element_type=F32)
             for pair in range(2)], axis=0) * LOG2E
        nl = jnp.maximum(y, 0.0) + jnp.log2(1.0 + jnp.exp2(-jnp.abs(y)))
        if diagonal:
            nl = jnp.where(causal, nl, 0.0)
        cs = jnp.dot(nl.astype(BF16), umat, preferred_element_type=F32)
        run = run_ref[...]
        csum = cs + run
        a = jnp.exp2(y - csum)
        if diagonal:
            a = jnp.where(causal, a, 0.0)
        a = a.astype(BF16)
        for pair in range(2):
            cols = slice(pair * LANES, (pair + 1) * LANES)
            pv = jnp.dot(a[2 * pair * TQ:(2 * pair + 2) * TQ, :], vt[:, cols], preferred_element_type=F32)
            acc_ref[:, cols] += jnp.where(lane_head == 0, pv[:TQ, :], pv[TQ:, :])
        run_ref[...] = run + cs[:, 0:1]

    tile(qb, True)

    def body(j, carry):
        tile(qb - 1 - j, False)
        return carry

    lax.fori_loop(0, qb, body, 0)
    o_ref[0] = acc_ref[...]


def _stickbreak(q, k, v, umat):
    b, s, w = q.shape
    return pl.pallas_call(
        _stickbreak_kernel,
        grid=(b, s // TQ),
        in_specs=[pl.BlockSpec((1, TQ, w), lambda bi, qi: (bi, qi, 0)),
                  pl.BlockSpec((1, s, w), lambda bi, qi: (bi, 0, 0)),
                  pl.BlockSpec((1, s, w), lambda bi, qi: (bi, 0, 0)),
                  pl.BlockSpec(umat.shape, lambda bi, qi: (0, 0))],
        out_specs=pl.BlockSpec((1, TQ, w), lambda bi, qi: (bi, qi, 0)),
        out_shape=jax.ShapeDtypeStruct((b, s, w), F32),
        scratch_shapes=[pltpu.VMEM((TQ, w), F32), pltpu.VMEM((HEADS_PER_MIXER * TQ, 1), F32)],
        compiler_params=_params("arbitrary", "arbitrary"),
    )(q, k, v, umat)


def _mixer_out_kernel(x_ref, oa_ref, ob_ref, oc_ref, od_ref, gho_ref, wout_ref, gffn_ref, bd_ref,
                      x1_ref, h2_ref):
    bd = bd_ref[...]
    acc = x_ref[...]
    for p, r in enumerate((oa_ref, ob_ref, oc_ref, od_ref)):
        cols = slice(p * MIX_W, (p + 1) * MIX_W)
        y = _head_rms(r[...], gho_ref[:, cols], bd).astype(BF16)
        acc = acc + jnp.dot(y, wout_ref[cols, :], preferred_element_type=F32)
    x1_ref[...] = acc
    h2_ref[...] = _row_rms(acc, gffn_ref[...]).astype(BF16)


def _mixer_out(x2d, oa, ob, oc, od, gho, wout, gffn, bd):
    t, d = x2d.shape
    full = lambda a: pl.BlockSpec(a.shape, lambda i: (0,) * a.ndim)
    tok = lambda w: pl.BlockSpec((TM, w), lambda i: (i, 0))
    return pl.pallas_call(
        _mixer_out_kernel,
        grid=(t // TM,),
        in_specs=[tok(d)] + [tok(MIX_W)] * 4 + [full(gho), full(wout), full(gffn), full(bd)],
        out_specs=[tok(d), tok(d)],
        out_shape=[jax.ShapeDtypeStruct((t, d), F32), jax.ShapeDtypeStruct((t, d), BF16)],
        compiler_params=_params("arbitrary"),
    )(x2d, oa, ob, oc, od, gho, wout, gffn, bd)


def _top16_rows(s):
    kidx = lax.broadcasted_iota(jnp.int32, s.shape, 0).astype(F32)
    rank = jnp.full(s.shape, float(PEER_TOPK), F32)
    vals = []
    for r in range(PEER_TOPK):
        m = jnp.max(s, axis=0, keepdims=True)
        first = jnp.min(jnp.where(s == m, kidx, float(PEER_N_KEYS)), axis=0, keepdims=True)
        hit = kidx == first
        rank = jnp.where(hit, float(r), rank)
        s = jnp.where(hit, -jnp.inf, s)
        vals.append(m)
    return vals, rank


def _top16_rows_unique(s, want_rank):
    rank = jnp.full(s.shape, float(PEER_TOPK), F32) if want_rank else None
    vals = []
    for r in range(PEER_TOPK):
        m = jnp.max(s, axis=0, keepdims=True)
        hit = s == m
        if want_rank:
            rank = jnp.where(hit, float(r), rank)
        s = jnp.where(hit, -jnp.inf, s)
        vals.append(m)
    hits = jnp.sum(jnp.where(s == -jnp.inf, 1.0, 0.0), axis=0, keepdims=True)
    return vals, rank, hits


def _peer_route_kernel(h2_ref, wqt_ref, keys_ref, r2_ref, e2_ref, n_ref, e1_ref, st_ref):
    tm = h2_ref.shape[0]
    qt = lax.dot_general(wqt_ref[...], h2_ref[...], _NT, preferred_element_type=F32).astype(BF16)
    for hp in range(2 * PEER_HEADS):
        rows = slice(hp * PEER_N_KEYS, (hp + 1) * PEER_N_KEYS)
        st_ref[hp] = jnp.dot(keys_ref[hp], qt[rows, :], preferred_element_type=F32)

    nlb = tm // LANES
    sub = lax.broadcasted_iota(jnp.int32, (8, LANES), 0).astype(F32)
    pos = jnp.concatenate([sub, sub + 8.0] + [sub + 16.0 * a for a in range(1, 8)] + [(sub + 8.0) * 16.0], axis=0)
    sub16 = lax.broadcasted_iota(jnp.int32, (PEER_TOPK, LANES), 0).astype(F32)

    def candidates(v1, v2):
        sv1 = jnp.zeros((PEER_TOPK, LANES), F32)
        sv2 = jnp.zeros((PEER_TOPK, LANES), F32)
        for r in range(PEER_TOPK):
            sv1 = jnp.where(sub16 == float(r), v1[r], sv1)
            sv2 = jnp.where(sub16 == float(r), v2[r], sv2)
        return jnp.concatenate(
            [v1[0] + sv2[0:8], v1[0] + sv2[8:16]] + [v1[a] + sv2[0:8] for a in range(1, 8)] + [sv1[8:16] + v2[0]],
            axis=0)

    def store(hd, lb, s1, s2, v1, v2, rk2, cnt, zsum):
        r2_ref[hd, lb] = rk2.astype(BF16)
        e2_ref[hd, lb] = jnp.exp(s2 - v2[0]).astype(BF16)
        n_ref[hd, lb] = cnt
        e1_ref[hd, lb] = jnp.exp(s1 - v1[0]) * (1.0 / zsum)

    def fast(hd, lb):
        lo = pl.multiple_of(lb * LANES, LANES)
        s1 = st_ref[2 * hd, :, pl.ds(lo, LANES)]
        s2 = st_ref[2 * hd + 1, :, pl.ds(lo, LANES)]

        v1, _, hits1 = _top16_rows_unique(s1, False)
        v2, rk2, hits2 = _top16_rows_unique(s2, True)
        cand = candidates(v1, v2)
        zsum = jnp.zeros((1, LANES), F32)
        top0 = v1[0] + v2[0]
        for r in range(PEER_TOPK):
            m = jnp.max(cand, axis=0, keepdims=True)
            cand = jnp.where(cand == m, -jnp.inf, cand)
            zsum = zsum + jnp.exp(m - top0)
        sel = jnp.where(cand == -jnp.inf, 1.0, 0.0)
        na = [jnp.sum(sel[0:16], axis=0, keepdims=True)]
        na += [jnp.sum(sel[8 + 8 * a:16 + 8 * a], axis=0, keepdims=True) for a in range(1, 8)]
        last = sel[72:80]
        cnt = jnp.zeros((PEER_N_KEYS, LANES), F32)
        for a in range(8):
            cnt = jnp.where(s1 == v1[a], na[a], cnt)
        for a in range(8, PEER_TOPK):
            cnt = jnp.where(s1 == v1[a], last[a - 8:a - 7], cnt)
        store(hd, lb, s1, s2, v1, v2, rk2, cnt, zsum)
        picked = jnp.sum(sel, axis=0, keepdims=True)
        return jnp.max(jnp.maximum(jnp.maximum(hits1, hits2), picked))

    def exact(hd, lb):
        lo = pl.multiple_of(lb * LANES, LANES)
        s1 = st_ref[2 * hd, :, pl.ds(lo, LANES)]
        s2 = st_ref[2 * hd + 1, :, pl.ds(lo, LANES)]
        v1, rk1 = _top16_rows(s1)
        v2, rk2 = _top16_rows(s2)
        cand = candidates(v1, v2)
        cnt = jnp.zeros((PEER_N_KEYS, LANES), F32)
        zsum = jnp.zeros((1, LANES), F32)
        top0 = v1[0] + v2[0]
        for r in range(PEER_TOPK):
            m = jnp.max(cand, axis=0, keepdims=True)
            pr = jnp.min(jnp.where(cand == m, pos, 4096.0), axis=0, keepdims=True)
            cand = jnp.where(pos == pr, -jnp.inf, cand)
            cnt = cnt + jnp.where(rk1 == jnp.floor(pr * (1.0 / 16.0)), 1.0, 0.0)
            zsum = zsum + jnp.exp(m - top0)
        store(hd, lb, s1, s2, v1, v2, rk2, cnt, zsum)

    def body(it, carry):
        per = nlb // ROUTE_BLOCKS_PER_TRIP
        hd = it // per
        lb0 = ROUTE_BLOCKS_PER_TRIP * (it % per)
        worst = [fast(hd, lb0 + k) for k in range(ROUTE_BLOCKS_PER_TRIP)]
        for k in range(ROUTE_BLOCKS_PER_TRIP):
            pl.when(worst[k] > PEER_TOPK + 0.5)(functools.partial(exact, hd, lb0 + k))
        return carry

    lax.fori_loop(0, PEER_HEADS * nlb // ROUTE_BLOCKS_PER_TRIP, body, 0)


def _peer_route(h2, wqt, keys):
    t, d = h2.shape
    tabs = lambda dt: jax.ShapeDtypeStruct((PEER_HEADS, t // LANES, PEER_N_KEYS, LANES), dt)
    tab_spec = pl.BlockSpec((PEER_HEADS, TM // LANES, PEER_N_KEYS, LANES), lambda i: (0, i, 0, 0))
    return pl.pallas_call(
        _peer_route_kernel,
        grid=(t // TM,),
        in_specs=[pl.BlockSpec((TM, d), lambda i: (i, 0)),
                  pl.BlockSpec(wqt.shape, lambda i: (0, 0)),
                  pl.BlockSpec(keys.shape, lambda i: (0, 0, 0))],
        out_specs=[tab_spec] * 4,
        out_shape=[tabs(BF16), tabs(BF16), tabs(F32), tabs(F32)],
        scratch_shapes=[pltpu.VMEM((2 * PEER_HEADS, PEER_N_KEYS, TM), F32)],
        compiler_params=_params("arbitrary"),
    )(h2, wqt, keys)


def _peer_ffn_kernel(h2_ref, u_ref, vt_ref, r2_ref, e2_ref, n_ref, e1_ref, x1_ref, o_ref,
                     acc_ref, at_ref, wt_ref):
    c = pl.program_id(1)
    tm = h2_ref.shape[0]
    ec = u_ref.shape[0]
    ngroup = ec // (PEER_GROUP * PEER_N_KEYS)

    @pl.when(c == 0)
    def _():
        acc_ref[...] = jnp.zeros(acc_ref.shape, F32)

    at_ref[...] = lax.dot_general(u_ref[...], h2_ref[...], _NT, preferred_element_type=F32)

    def group(gi, carry):
        grp = pl.ds(pl.multiple_of((c * ngroup + gi) * PEER_GROUP, PEER_GROUP), PEER_GROUP)
        for k in range(PEER_GROUP):
            rows = pl.ds(pl.multiple_of((gi * PEER_GROUP + k) * PEER_N_KEYS, PEER_N_KEYS), PEER_N_KEYS)
            for lb in range(tm // LANES):
                cols = slice(lb * LANES, (lb + 1) * LANES)
                gw = jnp.zeros((PEER_N_KEYS, LANES), BF16)
                for hd in range(PEER_HEADS):
                    nrow = n_ref[hd, lb, grp, :].astype(BF16)[k:k + 1, :]
                    e1row = e1_ref[hd, lb, grp, :].astype(BF16)[k:k + 1, :]
                    margin = nrow - r2_ref[hd, lb]
                    gw = gw + jnp.maximum(jnp.minimum(e1row * e2_ref[hd, lb], margin), 0.0)
                wt_ref[rows, cols] = _gelu_tanh(at_ref[rows, cols]).astype(BF16) * gw
        return carry

    lax.fori_loop(0, ngroup, group, 0)
    acc_ref[...] += jnp.dot(vt_ref[...], wt_ref[...], preferred_element_type=F32)

    @pl.when(c == pl.num_programs(1) - 1)
    def _():
        o_ref[...] = x1_ref[...] + acc_ref[...].T


def _peer_ffn(h2, u, vt, r2, e2, n, e1, x1):
    t, d = h2.shape
    ne = u.shape[0]
    tab_spec = pl.BlockSpec((PEER_HEADS, TM // LANES, PEER_N_KEYS, LANES), lambda i, c: (0, i, 0, 0))
    return pl.pallas_call(
        _peer_ffn_kernel,
        grid=(t // TM, ne // PEER_EC),
        in_specs=[pl.BlockSpec((TM, d), lambda i, c: (i, 0)),
                  pl.BlockSpec((PEER_EC, d), lambda i, c: (c, 0)),
                  pl.BlockSpec((d, PEER_EC), lambda i, c: (0, c)),
                  tab_spec, tab_spec, tab_spec, tab_spec,
                  pl.BlockSpec((TM, d), lambda i, c: (i, 0))],
        out_specs=pl.BlockSpec((TM, d), lambda i, c: (i, 0)),
        out_shape=jax.ShapeDtypeStruct((t, d), F32),
        scratch_shapes=[pltpu.VMEM((d, TM), F32), pltpu.VMEM((PEER_EC, TM), F32), pltpu.VMEM((PEER_EC, TM), BF16)],
        compiler_params=_params("arbitrary", "arbitrary"),
    )(h2, u, vt, r2, e2, n, e1, x1)


def _tile_heads(g):
    return jnp.tile(g, HEADS_PER_MIXER)[None, :]


def kernel(x, g_mix_norm, w_in, w_sgu, b_sgu, g_sgu_v, g_q_dil, g_k_dil, conv_w, g_q_sb, g_k_sb, g_head_out,
           w_out, g_ffn_norm, w_peer_q, peer_sub_keys, peer_u, peer_v):
    b, s, d = x.shape
    depth = w_in.shape[0]
    assert d == 4 * MIX_W and s % TM == 0 and s >= DIL_WIN
    assert w_in.shape[2] == N_IN_PIECES * MIX_W

    hh = np.arange(MIX_W) // HEAD_DIM
    bd = jnp.asarray(hh[:, None] == hh[None, :], BF16)
    lc = jnp.asarray(_dilated_logcount_table())
    umat = jnp.asarray(_suffix_sum_matrix(), BF16)

    x2d = x.reshape(b * s, d)
    for l in range(depth):
        wsgu_cat = w_sgu[l].transpose(1, 0, 2).reshape(SGU_CHUNK, HEADS_PER_MIXER * SGU_CHUNK)
        bsgu_tile = jnp.repeat(b_sgu[l].T, HEAD_DIM, axis=1)
        oa, oc, dq, dk, dv, bq, bk, bv = _mixer_in(
            x2d, g_mix_norm[l][None, :], w_in[l].astype(BF16), wsgu_cat, bsgu_tile, g_sgu_v[l][None, :],
            _tile_heads(g_q_dil[l]), _tile_heads(g_k_dil[l]), conv_w[l], _tile_heads(g_q_sb[l]),
            _tile_heads(g_k_sb[l]), bd, seq=s)
        r3 = lambda a: a.reshape(b, s, MIX_W)
        ob = _dilated(r3(dq), r3(dk), r3(dv), lc).reshape(b * s, MIX_W)
        od = _stickbreak(r3(bq), r3(bk), r3(bv), umat).reshape(b * s, MIX_W)
        x1, h2 = _mixer_out(x2d, oa, ob, oc, od, g_head_out[l][None, :], w_out[l].astype(BF16),
                            g_ffn_norm[l][None, :], bd)
        wqt = w_peer_q[l].T.astype(BF16)
        keys = peer_sub_keys[l].reshape(2 * PEER_HEADS, PEER_N_KEYS, -1).astype(BF16)
        r2, e2, n, e1 = _peer_route(h2, wqt, keys)
        x2d = _peer_ffn(h2, peer_u[l].astype(BF16), peer_v[l].T.astype(BF16), r2, e2, n, e1, x1)
    return x2d.reshape(b, s, d)
```

```python
import functools

import numpy as np
import jax
import jax.numpy as jnp
from jax import lax
from jax.experimental import pallas as pl
from jax.experimental.pallas import tpu as pltpu

HEAD_DIM = 64
MIX_W = 256
HEADS_PER_MIXER = MIX_W // HEAD_DIM
N_IN_PIECES = 11
SGU_CHUNK = 128
RMS_EPS = 1e-6
NEG = -1e30
LOG2E = 1.4426950408889634
PEER_HEADS = 8
PEER_N_KEYS = 128
PEER_TOPK = 16
ALIBI_SLOPES = tuple(2.0 ** (-8.0 * (i + 1) / HEADS_PER_MIXER) for i in range(HEADS_PER_MIXER))

LANES = 128
SUBLANES = 8
TM = 512
TQ = 256
DIL_MAX_DIST = 2048
DIL_NWIN = DIL_MAX_DIST // TQ + 1
DIL_WIN = DIL_NWIN * TQ
DIL_SHORT_TILES = 4
PEER_EC = 2048
ROUTE_BLOCKS_PER_TRIP = 4
PEER_GROUP = SUBLANES
VMEM_LIMIT = 56 * 1024 * 1024

F32 = jnp.float32
BF16 = jnp.bfloat16
_NT = (((1,), (1,)), ((), ()))


def _params(*sem):
    return pltpu.CompilerParams(dimension_semantics=sem, vmem_limit_bytes=VMEM_LIMIT)


def _head_sumsq(x2, bd):
    hi = x2.astype(BF16)
    lo = (x2 - hi.astype(F32)).astype(BF16)
    return (jnp.dot(hi, bd, preferred_element_type=F32) + jnp.dot(lo, bd, preferred_element_type=F32))


def _head_rms(x, g, bd):
    ss = _head_sumsq(x * x, bd)
    return x * lax.rsqrt(ss * (1.0 / HEAD_DIM) + RMS_EPS) * g


def _gelu_tanh(x):
    k = 2.0 * 0.7978845608028654 * LOG2E
    t = x * x * (-0.044715 * k) - k
    return x / (1.0 + jnp.exp2(x * t))


def _row_rms(x, g):
    ms = jnp.mean(x * x, axis=-1, keepdims=True)
    return x * lax.rsqrt(ms + RMS_EPS) * g


def _mixer_in_kernel(x_ref, gmix_ref, win_ref, wsgu_ref, bsgu_ref, gv_ref, gqd_ref, gkd_ref, convw_ref,
                     gqs_ref, gks_ref, bd_ref,
                     oa_ref, oc_ref, dq_ref, dk_ref, dv_ref, bq_ref, bk_ref, bv_ref,
                     ybuf_ref, *, tiles_per_seq):
    i = pl.program_id(0)
    tm = x_ref.shape[0]
    h = _row_rms(x_ref[...], gmix_ref[...]).astype(BF16)
    bd = bd_ref[...]
    scale = HEAD_DIM ** -0.5

    def piece(p):
        return jnp.dot(h, win_ref[:, p * MIX_W:(p + 1) * MIX_W], preferred_element_type=F32)

    u = jax.nn.gelu(piece(0))
    v = _head_rms(jax.nn.gelu(piece(1)), gv_ref[...], bd)
    t_idx = lax.broadcasted_iota(jnp.int32, (SGU_CHUNK, HEADS_PER_MIXER * SGU_CHUNK), 0)
    s_idx = lax.broadcasted_iota(jnp.int32, (SGU_CHUNK, HEADS_PER_MIXER * SGU_CHUNK), 1) & (SGU_CHUNK - 1)
    wcat = jnp.where(s_idx <= t_idx, wsgu_ref[...], 0.0).astype(BF16)
    lane_head = lax.broadcasted_iota(jnp.int32, (SGU_CHUNK, MIX_W), 1) >> 6
    bias = bsgu_ref[...]
    for c in range(tm // SGU_CHUNK):
        rows = slice(c * SGU_CHUNK, (c + 1) * SGU_CHUNK)
        vc = v[rows, :]
        vstack = jnp.concatenate(
            [jnp.where(lane_head == hh, vc, 0.0).astype(BF16) for hh in range(HEADS_PER_MIXER)], axis=0)
        mixed = jnp.dot(wcat, vstack, preferred_element_type=F32) + bias
        oa_ref[rows, :] = u[rows, :] * mixed

    dq_ref[...] = (_head_rms(piece(2), gqd_ref[...], bd) * scale).astype(BF16)
    dk_ref[...] = _head_rms(piece(3), gkd_ref[...], bd).astype(BF16)
    dv_ref[...] = piece(4).astype(BF16)
    bq_ref[...] = (_head_rms(piece(8), gqs_ref[...], bd) * scale).astype(BF16)
    bk_ref[...] = _head_rms(piece(9), gks_ref[...], bd).astype(BF16)
    bv_ref[...] = piece(10).astype(BF16)

    cb = piece(5)
    y = piece(6) * piece(7)

    @pl.when(i % tiles_per_seq == 0)
    def _():
        ybuf_ref[0:SUBLANES, :] = jnp.zeros((SUBLANES, MIX_W), F32)

    ybuf_ref[SUBLANES:SUBLANES + tm, :] = y
    y1 = ybuf_ref[SUBLANES - 1:SUBLANES - 1 + tm, :]
    y2 = ybuf_ref[SUBLANES - 2:SUBLANES - 2 + tm, :]
    w = convw_ref[...]
    oc_ref[...] = cb * (w[0:1, :] * y2 + w[1:2, :] * y1 + w[2:3, :] * y)
    ybuf_ref[0:SUBLANES, :] = y[tm - SUBLANES:tm, :]


def _mixer_in(x2d, gmix, win, wsgu_cat, bsgu_tile, gv, gqd, gkd, convw, gqs, gks, bd, *, seq):
    t = x2d.shape[0]
    d = x2d.shape[1]
    full = lambda a: pl.BlockSpec(a.shape, lambda i: (0,) * a.ndim)
    tok = lambda w: pl.BlockSpec((TM, w), lambda i: (i, 0))
    consts = (gmix, win, wsgu_cat, bsgu_tile, gv, gqd, gkd, convw, gqs, gks, bd)
    out_shape = ([jax.ShapeDtypeStruct((t, MIX_W), F32)] * 2 + [jax.ShapeDtypeStruct((t, MIX_W), BF16)] * 6)
    return pl.pallas_call(
        functools.partial(_mixer_in_kernel, tiles_per_seq=seq // TM),
        grid=(t // TM,),
        in_specs=[tok(d)] + [full(a) for a in consts],
        out_specs=[tok(MIX_W)] * 8,
        out_shape=out_shape,
        scratch_shapes=[pltpu.VMEM((TM + SUBLANES, MIX_W), F32)],
        compiler_params=_params("arbitrary"),
    )(x2d, *consts)


def _dilated_logcount_table():
    n = 2 * DIL_NWIN - 1
    qi = np.arange(TQ)[:, None]
    kj = np.arange(TQ)[None, :]
    tiles = []
    for u in range(n):
        off = DIL_NWIN - 1 - u
        dd = off * TQ + qi - kj
        cnt = ((dd >= 0) & (dd <= 128)).astype(np.int64)
        cnt += ((dd >= 0) & (dd <= 512) & (dd % 4 == 0))
        cnt += ((dd >= 0) & (dd <= 2048) & (dd % 16 == 0))
        tiles.append(np.where(cnt > 0, np.log(np.maximum(cnt, 1).astype(np.float64)), NEG))
    return np.concatenate(tiles, axis=1).astype(np.float32)


def _dilated_kernel(q_ref, k_ref, v_ref, lc_ref, o_ref):
    qb = pl.program_id(1)
    o0 = jnp.minimum(qb, DIL_NWIN - 1)
    kstart = pl.multiple_of((qb - o0) * TQ, TQ)
    cstart = pl.multiple_of((DIL_NWIN - 1 - o0) * TQ, TQ)
    lane_head = lax.broadcasted_iota(jnp.int32, (TQ, LANES), 1) >> 6

    def attend(win):
        lcw = lc_ref[:, pl.ds(cstart, win)]
        qi = lax.broadcasted_iota(jnp.int32, (TQ, win), 0)
        col = lax.broadcasted_iota(jnp.int32, (TQ, win), 1)
        dist = (o0 * TQ + qi - col).astype(F32)
        q = q_ref[0]
        kw = k_ref[0, pl.ds(kstart, win), :]
        vw = v_ref[0, pl.ds(kstart, win), :]
        for pair in range(2):
            cols = slice(pair * LANES, (pair + 1) * LANES)
            qp, kp, vp = q[:, cols], kw[:, cols], vw[:, cols]
            outs = []
            for hh in range(2):
                qm = jnp.where(lane_head == hh, qp, jnp.zeros_like(qp))
                s = lax.dot_general(qm, kp, _NT, preferred_element_type=F32)
                s = s + (lcw - ALIBI_SLOPES[2 * pair + hh] * dist)
                m = jnp.max(s, axis=-1, keepdims=True)
                p = jnp.exp(s - m)
                l = jnp.sum(p, axis=-1, keepdims=True)
                o = jnp.dot(p.astype(BF16), vp, preferred_element_type=F32)
                outs.append(o / l)
            o_ref[0, :, cols] = jnp.where(lane_head == 0, outs[0], outs[1])

    @pl.when(qb < DIL_SHORT_TILES)
    def _():
        attend(DIL_SHORT_TILES * TQ)

    @pl.when(qb >= DIL_SHORT_TILES)
    def _():
        attend(DIL_WIN)


def _dilated(q, k, v, lc):
    b, s, w = q.shape
    return pl.pallas_call(
        _dilated_kernel,
        grid=(b, s // TQ),
        in_specs=[pl.BlockSpec((1, TQ, w), lambda bi, qi: (bi, qi, 0)),
                  pl.BlockSpec((1, s, w), lambda bi, qi: (bi, 0, 0)),
                  pl.BlockSpec((1, s, w), lambda bi, qi: (bi, 0, 0)),
                  pl.BlockSpec(lc.shape, lambda bi, qi: (0, 0))],
        out_specs=pl.BlockSpec((1, TQ, w), lambda bi, qi: (bi, qi, 0)),
        out_shape=jax.ShapeDtypeStruct((b, s, w), F32),
        compiler_params=_params("arbitrary", "arbitrary"),
    )(q, k, v, lc)


def _suffix_sum_matrix():
    j = np.arange(TQ)[:, None]
    s = np.arange(TQ)[None, :]
    return (j >= s).astype(np.float32)


def _stickbreak_kernel(q_ref, k_ref, v_ref, u_ref, o_ref, acc_ref, run_ref):
    qb = pl.program_id(1)
    q = q_ref[0]
    umat = u_ref[...]
    acc_ref[...] = jnp.zeros(acc_ref.shape, F32)
    run_ref[...] = jnp.zeros(run_ref.shape, F32)
    lane_head = lax.broadcasted_iota(jnp.int32, (TQ, LANES), 1) >> 6
    qstack = []
    for pair in range(2):
        qp = q[:, pair * LANES:(pair + 1) * LANES]
        qstack.append(jnp.concatenate(
            [jnp.where(lane_head == hh, qp, jnp.zeros_like(qp)) for hh in range(2)], axis=0))
    nh = HEADS_PER_MIXER
    row = lax.broadcasted_iota(jnp.int32, (nh * TQ, TQ), 0) & (TQ - 1)
    colk = lax.broadcasted_iota(jnp.int32, (nh * TQ, TQ), 1)
    causal = colk < row

    def tile(kb, diagonal):
        koff = pl.multiple_of(kb * TQ, TQ)
        kt = k_ref[0, pl.ds(koff, TQ), :]
        vt = v_ref[0, pl.ds(koff, TQ), :]
        y = jnp.concatenate(
            [lax.dot_general(qstack[pair], kt[:, pair * LANES:(pair + 1) * LANES], _NT, preferred_element_type=F32)
             for pair in range(2)], axis=0) * LOG2E
        nl = jnp.maximum(y, 0.0) + jnp.log2(1.0 + jnp.exp2(-jnp.abs(y)))
        if diagonal:
            nl = jnp.where(causal, nl, 0.0)
        cs = jnp.dot(nl.astype(BF16), umat, preferred_element_type=F32)
        run = run_ref[...]
        csum = cs + run
        a = jnp.exp2(y - csum)
        if diagonal:
            a = jnp.where(causal, a, 0.0)
        a = a.astype(BF16)
        for pair in range(2):
            cols = slice(pair * LANES, (pair + 1) * LANES)
            pv = jnp.dot(a[2 * pair * TQ:(2 * pair + 2) * TQ, :], vt[:, cols], preferred_element_type=F32)
            acc_ref[:, cols] += jnp.where(lane_head == 0, pv[:TQ, :], pv[TQ:, :])
        run_ref[...] = run + cs[:, 0:1]

    tile(qb, True)

    def body(j, carry):
        tile(qb - 1 - j, False)
        return carry

    lax.fori_loop(0, qb, body, 0)
    o_ref[0] = acc_ref[...]


def _stickbreak(q, k, v, umat):
    b, s, w = q.shape
    return pl.pallas_call(
        _stickbreak_kernel,
        grid=(b, s // TQ),
        in_specs=[pl.BlockSpec((1, TQ, w), lambda bi, qi: (bi, qi, 0)),
                  pl.BlockSpec((1, s, w), lambda bi, qi: (bi, 0, 0)),
                  pl.BlockSpec((1, s, w), lambda bi, qi: (bi, 0, 0)),
                  pl.BlockSpec(umat.shape, lambda bi, qi: (0, 0))],
        out_specs=pl.BlockSpec((1, TQ, w), lambda bi, qi: (bi, qi, 0)),
        out_shape=jax.ShapeDtypeStruct((b, s, w), F32),
        scratch_shapes=[pltpu.VMEM((TQ, w), F32), pltpu.VMEM((HEADS_PER_MIXER * TQ, 1), F32)],
        compiler_params=_params("arbitrary", "arbitrary"),
    )(q, k, v, umat)


def _mixer_out_kernel(x_ref, oa_ref, ob_ref, oc_ref, od_ref, gho_ref, wout_ref, gffn_ref, bd_ref,
                      x1_ref, h2_ref):
    bd = bd_ref[...]
    acc = x_ref[...]
    for p, r in enumerate((oa_ref, ob_ref, oc_ref, od_ref)):
        cols = slice(p * MIX_W, (p + 1) * MIX_W)
        y = _head_rms(r[...], gho_ref[:, cols], bd).astype(BF16)
        acc = acc + jnp.dot(y, wout_ref[cols, :], preferred_element_type=F32)
    x1_ref[...] = acc
    h2_ref[...] = _row_rms(acc, gffn_ref[...]).astype(BF16)


def _mixer_out(x2d, oa, ob, oc, od, gho, wout, gffn, bd):
    t, d = x2d.shape
    full = lambda a: pl.BlockSpec(a.shape, lambda i: (0,) * a.ndim)
    tok = lambda w: pl.BlockSpec((TM, w), lambda i: (i, 0))
    return pl.pallas_call(
        _mixer_out_kernel,
        grid=(t // TM,),
        in_specs=[tok(d)] + [tok(MIX_W)] * 4 + [full(gho), full(wout), full(gffn), full(bd)],
        out_specs=[tok(d), tok(d)],
        out_shape=[jax.ShapeDtypeStruct((t, d), F32), jax.ShapeDtypeStruct((t, d), BF16)],
        compiler_params=_params("arbitrary"),
    )(x2d, oa, ob, oc, od, gho, wout, gffn, bd)


def _top16_rows(s):
    kidx = lax.broadcasted_iota(jnp.int32, s.shape, 0).astype(F32)
    rank = jnp.full(s.shape, float(PEER_TOPK), F32)
    vals = []
    for r in range(PEER_TOPK):
        m = jnp.max(s, axis=0, keepdims=True)
        first = jnp.min(jnp.where(s == m, kidx, float(PEER_N_KEYS)), axis=0, keepdims=True)
        hit = kidx == first
        rank = jnp.where(hit, float(r), rank)
        s = jnp.where(hit, -jnp.inf, s)
        vals.append(m)
    return vals, rank


def _top16_rows_unique(s, want_rank):
    rank = jnp.full(s.shape, float(PEER_TOPK), F32) if want_rank else None
    vals = []
    for r in range(PEER_TOPK):
        m = jnp.max(s, axis=0, keepdims=True)
        hit = s == m
        if want_rank:
            rank = jnp.where(hit, float(r), rank)
        s = jnp.where(hit, -jnp.inf, s)
        vals.append(m)
    hits = jnp.sum(jnp.where(s == -jnp.inf, 1.0, 0.0), axis=0, keepdims=True)
    return vals, rank, hits


def _peer_route_kernel(h2_ref, wqt_ref, keys_ref, r2_ref, e2_ref, n_ref, e1_ref, st_ref):
    tm = h2_ref.shape[0]
    qt = lax.dot_general(wqt_ref[...], h2_ref[...], _NT, preferred_element_type=F32).astype(BF16)
    for hp in range(2 * PEER_HEADS):
        rows = slice(hp * PEER_N_KEYS, (hp + 1) * PEER_N_KEYS)
        st_ref[hp] = jnp.dot(keys_ref[hp], qt[rows, :], preferred_element_type=F32)

    nlb = tm // LANES
    sub = lax.broadcasted_iota(jnp.int32, (8, LANES), 0).astype(F32)
    pos = jnp.concatenate([sub, sub + 8.0] + [sub + 16.0 * a for a in range(1, 8)] + [(sub + 8.0) * 16.0], axis=0)
    sub16 = lax.broadcasted_iota(jnp.int32, (PEER_TOPK, LANES), 0).astype(F32)

    def candidates(v1, v2):
        sv1 = jnp.zeros((PEER_TOPK, LANES), F32)
        sv2 = jnp.zeros((PEER_TOPK, LANES), F32)
        for r in range(PEER_TOPK):
            sv1 = jnp.where(sub16 == float(r), v1[r], sv1)
            sv2 = jnp.where(sub16 == float(r), v2[r], sv2)
        return jnp.concatenate(
            [v1[0] + sv2[0:8], v1[0] + sv2[8:16]] + [v1[a] + sv2[0:8] for a in range(1, 8)] + [sv1[8:16] + v2[0]],
            axis=0)

    def store(hd, lb, s1, s2, v1, v2, rk2, cnt, zsum):
        r2_ref[hd, lb] = rk2.astype(BF16)
        e2_ref[hd, lb] = jnp.exp(s2 - v2[0]).astype(BF16)
        n_ref[hd, lb] = cnt
        e1_ref[hd, lb] = jnp.exp(s1 - v1[0]) * (1.0 / zsum)

    def fast(hd, lb):
        lo = pl.multiple_of(lb * LANES, LANES)
        s1 = st_ref[2 * hd, :, pl.ds(lo, LANES)]
        s2 = st_ref[2 * hd + 1, :, pl.ds(lo, LANES)]

        v1, _, hits1 = _top16_rows_unique(s1, False)
        v2, rk2, hits2 = _top16_rows_unique(s2, True)
        cand = candidates(v1, v2)
        zsum = jnp.zeros((1, LANES), F32)
        top0 = v1[0] + v2[0]
        for r in range(PEER_TOPK):
            m = jnp.max(cand, axis=0, keepdims=True)
            cand = jnp.where(cand == m, -jnp.inf, cand)
            zsum = zsum + jnp.exp(m - top0)
        sel = jnp.where(cand == -jnp.inf, 1.0, 0.0)
        na = [jnp.sum(sel[0:16], axis=0, keepdims=True)]
        na += [jnp.sum(sel[8 + 8 * a:16 + 8 * a], axis=0, keepdims=True) for a in range(1, 8)]
        last = sel[72:80]
        cnt = jnp.zeros((PEER_N_KEYS, LANES), F32)
        for a in range(8):
            cnt = jnp.where(s1 == v1[a], na[a], cnt)
        for a in range(8, PEER_TOPK):
            cnt = jnp.where(s1 == v1[a], last[a - 8:a - 7], cnt)
        store(hd, lb, s1, s2, v1, v2, rk2, cnt, zsum)
        picked = jnp.sum(sel, axis=0, keepdims=True)
        return jnp.max(jnp.maximum(jnp.maximum(hits1, hits2), picked))

    def exact(hd, lb):
        lo = pl.multiple_of(lb * LANES, LANES)
        s1 = st_ref[2 * hd, :, pl.ds(lo, LANES)]
        s2 = st_ref[2 * hd + 1, :, pl.ds(lo, LANES)]
        v1, rk1 = _top16_rows(s1)
        v2, rk2 = _top16_rows(s2)
        cand = candidates(v1, v2)
        cnt = jnp.zeros((PEER_N_KEYS, LANES), F32)
        zsum = jnp.zeros((1, LANES), F32)
        top0 = v1[0] + v2[0]
        for r in range(PEER_TOPK):
            m = jnp.max(cand, axis=0, keepdims=True)
            pr = jnp.min(jnp.where(cand == m, pos, 4096.0), axis=0, keepdims=True)
            cand = jnp.where(pos == pr, -jnp.inf, cand)
            cnt = cnt + jnp.where(rk1 == jnp.floor(pr * (1.0 / 16.0)), 1.0, 0.0)
            zsum = zsum + jnp.exp(m - top0)
        store(hd, lb, s1, s2, v1, v2, rk2, cnt, zsum)

    def body(it, carry):
        per = nlb // ROUTE_BLOCKS_PER_TRIP
        hd = it // per
        lb0 = ROUTE_BLOCKS_PER_TRIP * (it % per)
        worst = [fast(hd, lb0 + k) for k in range(ROUTE_BLOCKS_PER_TRIP)]
        for k in range(ROUTE_BLOCKS_PER_TRIP):
            pl.when(worst[k] > PEER_TOPK + 0.5)(functools.partial(exact, hd, lb0 + k))
        return carry

    lax.fori_loop(0, PEER_HEADS * nlb // ROUTE_BLOCKS_PER_TRIP, body, 0)


def _peer_route(h2, wqt, keys):
    t, d = h2.shape
    tabs = lambda dt: jax.ShapeDtypeStruct((PEER_HEADS, t // LANES, PEER_N_KEYS, LANES), dt)
    tab_spec = pl.BlockSpec((PEER_HEADS, TM // LANES, PEER_N_KEYS, LANES), lambda i: (0, i, 0, 0))
    return pl.pallas_call(
        _peer_route_kernel,
        grid=(t // TM,),
        in_specs=[pl.BlockSpec((TM, d), lambda i: (i, 0)),
                  pl.BlockSpec(wqt.shape, lambda i: (0, 0)),
                  pl.BlockSpec(keys.shape, lambda i: (0, 0, 0))],
        out_specs=[tab_spec] * 4,
        out_shape=[tabs(BF16), tabs(BF16), tabs(F32), tabs(F32)],
        scratch_shapes=[pltpu.VMEM((2 * PEER_HEADS, PEER_N_KEYS, TM), F32)],
        compiler_params=_params("arbitrary"),
    )(h2, wqt, keys)


def _peer_ffn_kernel(h2_ref, u_ref, vt_ref, r2_ref, e2_ref, n_ref, e1_ref, x1_ref, o_ref,
                     acc_ref, at_ref, wt_ref):
    c = pl.program_id(1)
    tm = h2_ref.shape[0]
    ec = u_ref.shape[0]
    ngroup = ec // (PEER_GROUP * PEER_N_KEYS)

    @pl.when(c == 0)
    def _():
        acc_ref[...] = jnp.zeros(acc_ref.shape, F32)

    at_ref[...] = lax.dot_general(u_ref[...], h2_ref[...], _NT, preferred_element_type=F32)

    def group(gi, carry):
        grp = pl.ds(pl.multiple_of((c * ngroup + gi) * PEER_GROUP, PEER_GROUP), PEER_GROUP)
        for k in range(PEER_GROUP):
            rows = pl.ds(pl.multiple_of((gi * PEER_GROUP + k) * PEER_N_KEYS, PEER_N_KEYS), PEER_N_KEYS)
            for lb in range(tm // LANES):
                cols = slice(lb * LANES, (lb + 1) * LANES)
                gw = jnp.zeros((PEER_N_KEYS, LANES), BF16)
                for hd in range(PEER_HEADS):
                    nrow = n_ref[hd, lb, grp, :].astype(BF16)[k:k + 1, :]
                    e1row = e1_ref[hd, lb, grp, :].astype(BF16)[k:k + 1, :]
                    margin = nrow - r2_ref[hd, lb]
                    gw = gw + jnp.maximum(jnp.minimum(e1row * e2_ref[hd, lb], margin), 0.0)
                wt_ref[rows, cols] = _gelu_tanh(at_ref[rows, cols]).astype(BF16) * gw
        return carry

    lax.fori_loop(0, ngroup, group, 0)
    acc_ref[...] += jnp.dot(vt_ref[...], wt_ref[...], preferred_element_type=F32)

    @pl.when(c == pl.num_programs(1) - 1)
    def _():
        o_ref[...] = x1_ref[...] + acc_ref[...].T


def _peer_ffn(h2, u, vt, r2, e2, n, e1, x1):
    t, d = h2.shape
    ne = u.shape[0]
    tab_spec = pl.BlockSpec((PEER_HEADS, TM // LANES, PEER_N_KEYS, LANES), lambda i, c: (0, i, 0, 0))
    return pl.pallas_call(
        _peer_ffn_kernel,
        grid=(t // TM, ne // PEER_EC),
        in_specs=[pl.BlockSpec((TM, d), lambda i, c: (i, 0)),
                  pl.BlockSpec((PEER_EC, d), lambda i, c: (c, 0)),
                  pl.BlockSpec((d, PEER_EC), lambda i, c: (0, c)),
                  tab_spec, tab_spec, tab_spec, tab_spec,
                  pl.BlockSpec((TM, d), lambda i, c: (i, 0))],
        out_specs=pl.BlockSpec((TM, d), lambda i, c: (i, 0)),
        out_shape=jax.ShapeDtypeStruct((t, d), F32),
        scratch_shapes=[pltpu.VMEM((d, TM), F32), pltpu.VMEM((PEER_EC, TM), F32), pltpu.VMEM((PEER_EC, TM), BF16)],
        compiler_params=_params("arbitrary", "arbitrary"),
    )(h2, u, vt, r2, e2, n, e1, x1)


def _tile_heads(g):
    return jnp.tile(g, HEADS_PER_MIXER)[None, :]


def kernel(x, g_mix_norm, w_in, w_sgu, b_sgu, g_sgu_v, g_q_dil, g_k_dil, conv_w, g_q_sb, g_k_sb, g_head_out,
           w_out, g_ffn_norm, w_peer_q, peer_sub_keys, peer_u, peer_v):
    b, s, d = x.shape
    depth = w_in.shape[0]
    assert d == 4 * MIX_W and s % TM == 0 and s >= DIL_WIN
    assert w_in.shape[2] == N_IN_PIECES * MIX_W

    hh = np.arange(MIX_W) // HEAD_DIM
    bd = jnp.asarray(hh[:, None] == hh[None, :], BF16)
    lc = jnp.asarray(_dilated_logcount_table())
    umat = jnp.asarray(_suffix_sum_matrix(), BF16)

    x2d = x.reshape(b * s, d)
    for l in range(depth):
        wsgu_cat = w_sgu[l].transpose(1, 0, 2).reshape(SGU_CHUNK, HEADS_PER_MIXER * SGU_CHUNK)
        bsgu_tile = jnp.repeat(b_sgu[l].T, HEAD_DIM, axis=1)
        oa, oc, dq, dk, dv, bq, bk, bv = _mixer_in(
            x2d, g_mix_norm[l][None, :], w_in[l].astype(BF16), wsgu_cat, bsgu_tile, g_sgu_v[l][None, :],
            _tile_heads(g_q_dil[l]), _tile_heads(g_k_dil[l]), conv_w[l], _tile_heads(g_q_sb[l]),
            _tile_heads(g_k_sb[l]), bd, seq=s)
        r3 = lambda a: a.reshape(b, s, MIX_W)
        ob = _dilated(r3(dq), r3(dk), r3(dv), lc).reshape(b * s, MIX_W)
        od = _stickbreak(r3(bq), r3(bk), r3(bv), umat).reshape(b * s, MIX_W)
        x1, h2 = _mixer_out(x2d, oa, ob, oc, od, g_head_out[l][None, :], w_out[l].astype(BF16),
                            g_ffn_norm[l][None, :], bd)
        wqt = w_peer_q[l].T.astype(BF16)
        keys = peer_sub_keys[l].reshape(2 * PEER_HEADS, PEER_N_KEYS, -1).astype(BF16)
        r2, e2, n, e1 = _peer_route(h2, wqt, keys)
        x2d = _peer_ffn(h2, peer_u[l].astype(BF16), peer_v[l].T.astype(BF16), r2, e2, n, e1, x1)
    return x2d.reshape(b, s, d)
```

```python
import functools

import numpy as np
import jax
import jax.numpy as jnp
from jax import lax
from jax.experimental import pallas as pl
from jax.experimental.pallas import tpu as pltpu

HEAD_DIM = 64
MIX_W = 256
HEADS_PER_MIXER = MIX_W // HEAD_DIM
N_IN_PIECES = 11
SGU_CHUNK = 128
RMS_EPS = 1e-6
NEG = -1e30
LOG2E = 1.4426950408889634
PEER_HEADS = 8
PEER_N_KEYS = 128
PEER_TOPK = 16
ALIBI_SLOPES = tuple(2.0 ** (-8.0 * (i + 1) / HEADS_PER_MIXER) for i in range(HEADS_PER_MIXER))

LANES = 128
SUBLANES = 8
TM = 512
TQ = 256
DIL_MAX_DIST = 2048
DIL_NWIN = DIL_MAX_DIST // TQ + 1
DIL_WIN = DIL_NWIN * TQ
DIL_SHORT_TILES = 4
PEER_EC = 2048
ROUTE_BLOCKS_PER_TRIP = 4
PEER_GROUP = SUBLANES
VMEM_LIMIT = 56 * 1024 * 1024

F32 = jnp.float32
BF16 = jnp.bfloat16
_NT = (((1,), (1,)), ((), ()))


def _params(*sem):
    return pltpu.CompilerParams(dimension_semantics=sem, vmem_limit_bytes=VMEM_LIMIT)


def _head_sumsq(x2, bd):
    hi = x2.astype(BF16)
    lo = (x2 - hi.astype(F32)).astype(BF16)
    return (jnp.dot(hi, bd, preferred_element_type=F32) + jnp.dot(lo, bd, preferred_element_type=F32))


def _head_rms(x, g, bd):
    ss = _head_sumsq(x * x, bd)
    return x * lax.rsqrt(ss * (1.0 / HEAD_DIM) + RMS_EPS) * g


def _gelu_tanh(x):
    k = 2.0 * 0.7978845608028654 * LOG2E
    t = x * x * (-0.044715 * k) - k
    return x / (1.0 + jnp.exp2(x * t))


def _row_rms(x, g):
    ms = jnp.mean(x * x, axis=-1, keepdims=True)
    return x * lax.rsqrt(ms + RMS_EPS) * g


def _mixer_in_kernel(x_ref, gmix_ref, win_ref, wsgu_ref, bsgu_ref, gv_ref, gqd_ref, gkd_ref, convw_ref,
                     gqs_ref, gks_ref, bd_ref,
                     oa_ref, oc_ref, dq_ref, dk_ref, dv_ref, bq_ref, bk_ref, bv_ref,
                     ybuf_ref, *, tiles_per_seq):
    i = pl.program_id(0)
    tm = x_ref.shape[0]
    h = _row_rms(x_ref[...], gmix_ref[...]).astype(BF16)
    bd = bd_ref[...]
    scale = HEAD_DIM ** -0.5

    def piece(p):
        return jnp.dot(h, win_ref[:, p * MIX_W:(p + 1) * MIX_W], preferred_element_type=F32)

    u = jax.nn.gelu(piece(0))
    v = _head_rms(jax.nn.gelu(piece(1)), gv_ref[...], bd)
    t_idx = lax.broadcasted_iota(jnp.int32, (SGU_CHUNK, HEADS_PER_MIXER * SGU_CHUNK), 0)
    s_idx = lax.broadcasted_iota(jnp.int32, (SGU_CHUNK, HEADS_PER_MIXER * SGU_CHUNK), 1) & (SGU_CHUNK - 1)
    wcat = jnp.where(s_idx <= t_idx, wsgu_ref[...], 0.0).astype(BF16)
    lane_head = lax.broadcasted_iota(jnp.int32, (SGU_CHUNK, MIX_W), 1) >> 6
    bias = bsgu_ref[...]
    for c in range(tm // SGU_CHUNK):
        rows = slice(c * SGU_CHUNK, (c + 1) * SGU_CHUNK)
        vc = v[rows, :]
        vstack = jnp.concatenate(
            [jnp.where(lane_head == hh, vc, 0.0).astype(BF16) for hh in range(HEADS_PER_MIXER)], axis=0)
        mixed = jnp.dot(wcat, vstack, preferred_element_type=F32) + bias
        oa_ref[rows, :] = u[rows, :] * mixed

    dq_ref[...] = (_head_rms(piece(2), gqd_ref[...], bd) * scale).astype(BF16)
    dk_ref[...] = _head_rms(piece(3), gkd_ref[...], bd).astype(BF16)
    dv_ref[...] = piece(4).astype(BF16)
    bq_ref[...] = (_head_rms(piece(8), gqs_ref[...], bd) * scale).astype(BF16)
    bk_ref[...] = _head_rms(piece(9), gks_ref[...], bd).astype(BF16)
    bv_ref[...] = piece(10).astype(BF16)

    cb = piece(5)
    y = piece(6) * piece(7)

    @pl.when(i % tiles_per_seq == 0)
    def _():
        ybuf_ref[0:SUBLANES, :] = jnp.zeros((SUBLANES, MIX_W), F32)

    ybuf_ref[SUBLANES:SUBLANES + tm, :] = y
    y1 = ybuf_ref[SUBLANES - 1:SUBLANES - 1 + tm, :]
    y2 = ybuf_ref[SUBLANES - 2:SUBLANES - 2 + tm, :]
    w = convw_ref[...]
    oc_ref[...] = cb * (w[0:1, :] * y2 + w[1:2, :] * y1 + w[2:3, :] * y)
    ybuf_ref[0:SUBLANES, :] = y[tm - SUBLANES:tm, :]


def _mixer_in(x2d, gmix, win, wsgu_cat, bsgu_tile, gv, gqd, gkd, convw, gqs, gks, bd, *, seq):
    t = x2d.shape[0]
    d = x2d.shape[1]
    full = lambda a: pl.BlockSpec(a.shape, lambda i: (0,) * a.ndim)
    tok = lambda w: pl.BlockSpec((TM, w), lambda i: (i, 0))
    consts = (gmix, win, wsgu_cat, bsgu_tile, gv, gqd, gkd, convw, gqs, gks, bd)
    out_shape = ([jax.ShapeDtypeStruct((t, MIX_W), F32)] * 2 + [jax.ShapeDtypeStruct((t, MIX_W), BF16)] * 6)
    return pl.pallas_call(
        functools.partial(_mixer_in_kernel, tiles_per_seq=seq // TM),
        grid=(t // TM,),
        in_specs=[tok(d)] + [full(a) for a in consts],
        out_specs=[tok(MIX_W)] * 8,
        out_shape=out_shape,
        scratch_shapes=[pltpu.VMEM((TM + SUBLANES, MIX_W), F32)],
        compiler_params=_params("arbitrary"),
    )(x2d, *consts)


def _dilated_logcount_table():
    n = 2 * DIL_NWIN - 1
    qi = np.arange(TQ)[:, None]
    kj = np.arange(TQ)[None, :]
    tiles = []
    for u in range(n):
        off = DIL_NWIN - 1 - u
        dd = off * TQ + qi - kj
        cnt = ((dd >= 0) & (dd <= 128)).astype(np.int64)
        cnt += ((dd >= 0) & (dd <= 512) & (dd % 4 == 0))
        cnt += ((dd >= 0) & (dd <= 2048) & (dd % 16 == 0))
        tiles.append(np.where(cnt > 0, np.log(np.maximum(cnt, 1).astype(np.float64)), NEG))
    return np.concatenate(tiles, axis=1).astype(np.float32)


def _dilated_kernel(q_ref, k_ref, v_ref, lc_ref, o_ref):
    qb = pl.program_id(1)
    o0 = jnp.minimum(qb, DIL_NWIN - 1)
    kstart = pl.multiple_of((qb - o0) * TQ, TQ)
    cstart = pl.multiple_of((DIL_NWIN - 1 - o0) * TQ, TQ)
    lane_head = lax.broadcasted_iota(jnp.int32, (TQ, LANES), 1) >> 6

    def attend(win):
        lcw = lc_ref[:, pl.ds(cstart, win)]
        qi = lax.broadcasted_iota(jnp.int32, (TQ, win), 0)
        col = lax.broadcasted_iota(jnp.int32, (TQ, win), 1)
        dist = (o0 * TQ + qi - col).astype(F32)
        q = q_ref[0]
        kw = k_ref[0, pl.ds(kstart, win), :]
        vw = v_ref[0, pl.ds(kstart, win), :]
        for pair in range(2):
            cols = slice(pair * LANES, (pair + 1) * LANES)
            qp, kp, vp = q[:, cols], kw[:, cols], vw[:, cols]
            outs = []
            for hh in range(2):
                qm = jnp.where(lane_head == hh, qp, jnp.zeros_like(qp))
                s = lax.dot_general(qm, kp, _NT, preferred_element_type=F32)
                s = s + (lcw - ALIBI_SLOPES[2 * pair + hh] * dist)
                m = jnp.max(s, axis=-1, keepdims=True)
                p = jnp.exp(s - m)
                l = jnp.sum(p, axis=-1, keepdims=True)
                o = jnp.dot(p.astype(BF16), vp, preferred_element_type=F32)
                outs.append(o / l)
            o_ref[0, :, cols] = jnp.where(lane_head == 0, outs[0], outs[1])

    @pl.when(qb < DIL_SHORT_TILES)
    def _():
        attend(DIL_SHORT_TILES * TQ)

    @pl.when(qb >= DIL_SHORT_TILES)
    def _():
        attend(DIL_WIN)


def _dilated(q, k, v, lc):
    b, s, w = q.shape
    return pl.pallas_call(
        _dilated_kernel,
        grid=(b, s // TQ),
        in_specs=[pl.BlockSpec((1, TQ, w), lambda bi, qi: (bi, qi, 0)),
                  pl.BlockSpec((1, s, w), lambda bi, qi: (bi, 0, 0)),
                  pl.BlockSpec((1, s, w), lambda bi, qi: (bi, 0, 0)),
                  pl.BlockSpec(lc.shape, lambda bi, qi: (0, 0))],
        out_specs=pl.BlockSpec((1, TQ, w), lambda bi, qi: (bi, qi, 0)),
        out_shape=jax.ShapeDtypeStruct((b, s, w), F32),
        compiler_params=_params("arbitrary", "arbitrary"),
    )(q, k, v, lc)


def _suffix_sum_matrix():
    j = np.arange(TQ)[:, None]
    s = np.arange(TQ)[None, :]
    return (j >= s).astype(np.float32)


def _stickbreak_kernel(q_ref, k_ref, v_ref, u_ref, o_ref, acc_ref, run_ref):
    qb = pl.program_id(1)
    q = q_ref[0]
    umat = u_ref[...]
    acc_ref[...] = jnp.zeros(acc_ref.shape, F32)
    run_ref[...] = jnp.zeros(run_ref.shape, F32)
    lane_head = lax.broadcasted_iota(jnp.int32, (TQ, LANES), 1) >> 6
    qstack = []
    for pair in range(2):
        qp = q[:, pair * LANES:(pair + 1) * LANES]
        qstack.append(jnp.concatenate(
            [jnp.where(lane_head == hh, qp, jnp.zeros_like(qp)) for hh in range(2)], axis=0))
    nh = HEADS_PER_MIXER
    row = lax.broadcasted_iota(jnp.int32, (nh * TQ, TQ), 0) & (TQ - 1)
    colk = lax.broadcasted_iota(jnp.int32, (nh * TQ, TQ), 1)
    causal = colk < row

    def scores(kb, diagonal):
        kt = k_ref[0, pl.ds(pl.multiple_of(kb * TQ, TQ), TQ), :]
        y = jnp.concatenate(
            [lax.dot_general(qstack[pair], kt[:, pair * LANES:(pair + 1) * LANES], _NT, preferred_element_type=F32)
             for pair in range(2)], axis=0) * LOG2E
        nl = jnp.maximum(y, 0.0) + jnp.log2(1.0 + jnp.exp2(-jnp.abs(y)))
        if diagonal:
            nl = jnp.where(causal, nl, 0.0)
        cs = jnp.dot(nl.astype(BF16), umat, preferred_element_type=F32)
        return y, cs

    def values(kb, diagonal, y, cs):
        vt = v_ref[0, pl.ds(pl.multiple_of(kb * TQ, TQ), TQ), :]
        run = run_ref[...]
        a = jnp.exp2(y - (cs + run))
        if diagonal:
            a = jnp.where(causal, a, 0.0)
        a = a.astype(BF16)
        for pair in range(2):
            cols = slice(pair * LANES, (pair + 1) * LANES)
            pv = jnp.dot(a[2 * pair * TQ:(2 * pair + 2) * TQ, :], vt[:, cols], preferred_element_type=F32)
            acc_ref[:, cols] += jnp.where(lane_head == 0, pv[:TQ, :], pv[TQ:, :])
        run_ref[...] = run + cs[:, 0:1]

    values(qb, True, *scores(qb, True))

    @pl.when(qb % 2 == 1)
    def _():
        values(qb - 1, False, *scores(qb - 1, False))

    def body(j, carry):
        k1 = qb - (qb % 2) - 1 - 2 * j
        s1 = scores(k1, False)
        s2 = scores(k1 - 1, False)
        values(k1, False, *s1)
        values(k1 - 1, False, *s2)
        return carry

    lax.fori_loop(0, qb // 2, body, 0)
    o_ref[0] = acc_ref[...]


def _stickbreak(q, k, v, umat):
    b, s, w = q.shape
    return pl.pallas_call(
        _stickbreak_kernel,
        grid=(b, s // TQ),
        in_specs=[pl.BlockSpec((1, TQ, w), lambda bi, qi: (bi, qi, 0)),
                  pl.BlockSpec((1, s, w), lambda bi, qi: (bi, 0, 0)),
                  pl.BlockSpec((1, s, w), lambda bi, qi: (bi, 0, 0)),
                  pl.BlockSpec(umat.shape, lambda bi, qi: (0, 0))],
        out_specs=pl.BlockSpec((1, TQ, w), lambda bi, qi: (bi, qi, 0)),
        out_shape=jax.ShapeDtypeStruct((b, s, w), F32),
        scratch_shapes=[pltpu.VMEM((TQ, w), F32), pltpu.VMEM((HEADS_PER_MIXER * TQ, 1), F32)],
        compiler_params=_params("arbitrary", "arbitrary"),
    )(q, k, v, umat)


def _mixer_out_kernel(x_ref, oa_ref, ob_ref, oc_ref, od_ref, gho_ref, wout_ref, gffn_ref, bd_ref,
                      x1_ref, h2_ref):
    bd = bd_ref[...]
    acc = x_ref[...]
    for p, r in enumerate((oa_ref, ob_ref, oc_ref, od_ref)):
        cols = slice(p * MIX_W, (p + 1) * MIX_W)
        y = _head_rms(r[...], gho_ref[:, cols], bd).astype(BF16)
        acc = acc + jnp.dot(y, wout_ref[cols, :], preferred_element_type=F32)
    x1_ref[...] = acc
    h2_ref[...] = _row_rms(acc, gffn_ref[...]).astype(BF16)


def _mixer_out(x2d, oa, ob, oc, od, gho, wout, gffn, bd):
    t, d = x2d.shape
    full = lambda a: pl.BlockSpec(a.shape, lambda i: (0,) * a.ndim)
    tok = lambda w: pl.BlockSpec((TM, w), lambda i: (i, 0))
    return pl.pallas_call(
        _mixer_out_kernel,
        grid=(t // TM,),
        in_specs=[tok(d)] + [tok(MIX_W)] * 4 + [full(gho), full(wout), full(gffn), full(bd)],
        out_specs=[tok(d), tok(d)],
        out_shape=[jax.ShapeDtypeStruct((t, d), F32), jax.ShapeDtypeStruct((t, d), BF16)],
        compiler_params=_params("arbitrary"),
    )(x2d, oa, ob, oc, od, gho, wout, gffn, bd)


def _top16_rows(s):
    kidx = lax.broadcasted_iota(jnp.int32, s.shape, 0).astype(F32)
    rank = jnp.full(s.shape, float(PEER_TOPK), F32)
    vals = []
    for r in range(PEER_TOPK):
        m = jnp.max(s, axis=0, keepdims=True)
        first = jnp.min(jnp.where(s == m, kidx, float(PEER_N_KEYS)), axis=0, keepdims=True)
        hit = kidx == first
        rank = jnp.where(hit, float(r), rank)
        s = jnp.where(hit, -jnp.inf, s)
        vals.append(m)
    return vals, rank


def _top16_rows_unique(s, want_rank):
    rank = jnp.full(s.shape, float(PEER_TOPK), F32) if want_rank else None
    vals = []
    for r in range(PEER_TOPK):
        m = jnp.max(s, axis=0, keepdims=True)
        hit = s == m
        if want_rank:
            rank = jnp.where(hit, float(r), rank)
        s = jnp.where(hit, -jnp.inf, s)
        vals.append(m)
    hits = jnp.sum(jnp.where(s == -jnp.inf, 1.0, 0.0), axis=0, keepdims=True)
    return vals, rank, hits


def _peer_route_kernel(h2_ref, wqt_ref, keys_ref, r2_ref, e2_ref, n_ref, e1_ref, st_ref):
    tm = h2_ref.shape[0]
    qt = lax.dot_general(wqt_ref[...], h2_ref[...], _NT, preferred_element_type=F32).astype(BF16)
    for hp in range(2 * PEER_HEADS):
        rows = slice(hp * PEER_N_KEYS, (hp + 1) * PEER_N_KEYS)
        st_ref[hp] = jnp.dot(keys_ref[hp], qt[rows, :], preferred_element_type=F32)

    nlb = tm // LANES
    sub = lax.broadcasted_iota(jnp.int32, (8, LANES), 0).astype(F32)
    pos = jnp.concatenate([sub, sub + 8.0] + [sub + 16.0 * a for a in range(1, 8)] + [(sub + 8.0) * 16.0], axis=0)
    sub16 = lax.broadcasted_iota(jnp.int32, (PEER_TOPK, LANES), 0).astype(F32)

    def candidates(v1, v2):
        sv1 = jnp.zeros((PEER_TOPK, LANES), F32)
        sv2 = jnp.zeros((PEER_TOPK, LANES), F32)
        for r in range(PEER_TOPK):
            sv1 = jnp.where(sub16 == float(r), v1[r], sv1)
            sv2 = jnp.where(sub16 == float(r), v2[r], sv2)
        return jnp.concatenate(
            [v1[0] + sv2[0:8], v1[0] + sv2[8:16]] + [v1[a] + sv2[0:8] for a in range(1, 8)] + [sv1[8:16] + v2[0]],
            axis=0)

    def store(hd, lb, s1, s2, v1, v2, rk2, cnt, zsum):
        r2_ref[hd, lb] = rk2.astype(BF16)
        e2_ref[hd, lb] = jnp.exp(s2 - v2[0]).astype(BF16)
        n_ref[hd, lb] = cnt
        e1_ref[hd, lb] = jnp.exp(s1 - v1[0]) * (1.0 / zsum)

    def fast(hd, lb):
        lo = pl.multiple_of(lb * LANES, LANES)
        s1 = st_ref[2 * hd, :, pl.ds(lo, LANES)]
        s2 = st_ref[2 * hd + 1, :, pl.ds(lo, LANES)]

        v1, _, hits1 = _top16_rows_unique(s1, False)
        v2, rk2, hits2 = _top16_rows_unique(s2, True)
        cand = candidates(v1, v2)
        zsum = jnp.zeros((1, LANES), F32)
        top0 = v1[0] + v2[0]
        for r in range(PEER_TOPK):
            m = jnp.max(cand, axis=0, keepdims=True)
            cand = jnp.where(cand == m, -jnp.inf, cand)
            zsum = zsum + jnp.exp(m - top0)
        sel = jnp.where(cand == -jnp.inf, 1.0, 0.0)
        na = [jnp.sum(sel[0:16], axis=0, keepdims=True)]
        na += [jnp.sum(sel[8 + 8 * a:16 + 8 * a], axis=0, keepdims=True) for a in range(1, 8)]
        last = sel[72:80]
        cnt = jnp.zeros((PEER_N_KEYS, LANES), F32)
        for a in range(8):
            cnt = jnp.where(s1 == v1[a], na[a], cnt)
        for a in range(8, PEER_TOPK):
            cnt = jnp.where(s1 == v1[a], last[a - 8:a - 7], cnt)
        store(hd, lb, s1, s2, v1, v2, rk2, cnt, zsum)
        picked = jnp.sum(sel, axis=0, keepdims=True)
        return jnp.max(jnp.maximum(jnp.maximum(hits1, hits2), picked))

    def exact(hd, lb):
        lo = pl.multiple_of(lb * LANES, LANES)
        s1 = st_ref[2 * hd, :, pl.ds(lo, LANES)]
        s2 = st_ref[2 * hd + 1, :, pl.ds(lo, LANES)]
        v1, rk1 = _top16_rows(s1)
        v2, rk2 = _top16_rows(s2)
        cand = candidates(v1, v2)
        cnt = jnp.zeros((PEER_N_KEYS, LANES), F32)
        zsum = jnp.zeros((1, LANES), F32)
        top0 = v1[0] + v2[0]
        for r in range(PEER_TOPK):
            m = jnp.max(cand, axis=0, keepdims=True)
            pr = jnp.min(jnp.where(cand == m, pos, 4096.0), axis=0, keepdims=True)
            cand = jnp.where(pos == pr, -jnp.inf, cand)
            cnt = cnt + jnp.where(rk1 == jnp.floor(pr * (1.0 / 16.0)), 1.0, 0.0)
            zsum = zsum + jnp.exp(m - top0)
        store(hd, lb, s1, s2, v1, v2, rk2, cnt, zsum)

    def body(it, carry):
        per = nlb // ROUTE_BLOCKS_PER_TRIP
        hd = it // per
        lb0 = ROUTE_BLOCKS_PER_TRIP * (it % per)
        worst = [fast(hd, lb0 + k) for k in range(ROUTE_BLOCKS_PER_TRIP)]
        for k in range(ROUTE_BLOCKS_PER_TRIP):
            pl.when(worst[k] > PEER_TOPK + 0.5)(functools.partial(exact, hd, lb0 + k))
        return carry

    lax.fori_loop(0, PEER_HEADS * nlb // ROUTE_BLOCKS_PER_TRIP, body, 0)


def _peer_route(h2, wqt, keys):
    t, d = h2.shape
    tabs = lambda dt: jax.ShapeDtypeStruct((PEER_HEADS, t // LANES, PEER_N_KEYS, LANES), dt)
    tab_spec = pl.BlockSpec((PEER_HEADS, TM // LANES, PEER_N_KEYS, LANES), lambda i: (0, i, 0, 0))
    return pl.pallas_call(
        _peer_route_kernel,
        grid=(t // TM,),
        in_specs=[pl.BlockSpec((TM, d), lambda i: (i, 0)),
                  pl.BlockSpec(wqt.shape, lambda i: (0, 0)),
                  pl.BlockSpec(keys.shape, lambda i: (0, 0, 0))],
        out_specs=[tab_spec] * 4,
        out_shape=[tabs(BF16), tabs(BF16), tabs(F32), tabs(F32)],
        scratch_shapes=[pltpu.VMEM((2 * PEER_HEADS, PEER_N_KEYS, TM), F32)],
        compiler_params=_params("arbitrary"),
    )(h2, wqt, keys)


def _peer_ffn_kernel(h2_ref, u_ref, vt_ref, r2_ref, e2_ref, n_ref, e1_ref, x1_ref, o_ref,
                     acc_ref, at_ref, wt_ref):
    c = pl.program_id(1)
    tm = h2_ref.shape[0]
    ec = u_ref.shape[0]
    ngroup = ec // (PEER_GROUP * PEER_N_KEYS)

    @pl.when(c == 0)
    def _():
        acc_ref[...] = jnp.zeros(acc_ref.shape, F32)

    at_ref[...] = lax.dot_general(u_ref[...], h2_ref[...], _NT, preferred_element_type=F32)

    def group(gi, carry):
        grp = pl.ds(pl.multiple_of((c * ngroup + gi) * PEER_GROUP, PEER_GROUP), PEER_GROUP)
        for k in range(PEER_GROUP):
            rows = pl.ds(pl.multiple_of((gi * PEER_GROUP + k) * PEER_N_KEYS, PEER_N_KEYS), PEER_N_KEYS)
            for lb in range(tm // LANES):
                cols = slice(lb * LANES, (lb + 1) * LANES)
                gw = jnp.zeros((PEER_N_KEYS, LANES), BF16)
                for hd in range(PEER_HEADS):
                    nrow = n_ref[hd, lb, grp, :].astype(BF16)[k:k + 1, :]
                    e1row = e1_ref[hd, lb, grp, :].astype(BF16)[k:k + 1, :]
                    margin = nrow - r2_ref[hd, lb]
                    gw = gw + jnp.maximum(jnp.minimum(e1row * e2_ref[hd, lb], margin), 0.0)
                wt_ref[rows, cols] = _gelu_tanh(at_ref[rows, cols]).astype(BF16) * gw
        return carry

    lax.fori_loop(0, ngroup, group, 0)
    acc_ref[...] += jnp.dot(vt_ref[...], wt_ref[...], preferred_element_type=F32)

    @pl.when(c == pl.num_programs(1) - 1)
    def _():
        o_ref[...] = x1_ref[...] + acc_ref[...].T


def _peer_ffn(h2, u, vt, r2, e2, n, e1, x1):
    t, d = h2.shape
    ne = u.shape[0]
    tab_spec = pl.BlockSpec((PEER_HEADS, TM // LANES, PEER_N_KEYS, LANES), lambda i, c: (0, i, 0, 0))
    return pl.pallas_call(
        _peer_ffn_kernel,
        grid=(t // TM, ne // PEER_EC),
        in_specs=[pl.BlockSpec((TM, d), lambda i, c: (i, 0)),
                  pl.BlockSpec((PEER_EC, d), lambda i, c: (c, 0)),
                  pl.BlockSpec((d, PEER_EC), lambda i, c: (0, c)),
                  tab_spec, tab_spec, tab_spec, tab_spec,
                  pl.BlockSpec((TM, d), lambda i, c: (i, 0))],
        out_specs=pl.BlockSpec((TM, d), lambda i, c: (i, 0)),
        out_shape=jax.ShapeDtypeStruct((t, d), F32),
        scratch_shapes=[pltpu.VMEM((d, TM), F32), pltpu.VMEM((PEER_EC, TM), F32), pltpu.VMEM((PEER_EC, TM), BF16)],
        compiler_params=_params("arbitrary", "arbitrary"),
    )(h2, u, vt, r2, e2, n, e1, x1)


def _tile_heads(g):
    return jnp.tile(g, HEADS_PER_MIXER)[None, :]


def kernel(x, g_mix_norm, w_in, w_sgu, b_sgu, g_sgu_v, g_q_dil, g_k_dil, conv_w, g_q_sb, g_k_sb, g_head_out,
           w_out, g_ffn_norm, w_peer_q, peer_sub_keys, peer_u, peer_v):
    b, s, d = x.shape
    depth = w_in.shape[0]
    assert d == 4 * MIX_W and s % TM == 0 and s >= DIL_WIN
    assert w_in.shape[2] == N_IN_PIECES * MIX_W

    hh = np.arange(MIX_W) // HEAD_DIM
    bd = jnp.asarray(hh[:, None] == hh[None, :], BF16)
    lc = jnp.asarray(_dilated_logcount_table())
    umat = jnp.asarray(_suffix_sum_matrix(), BF16)

    x2d = x.reshape(b * s, d)
    for l in range(depth):
        wsgu_cat = w_sgu[l].transpose(1, 0, 2).reshape(SGU_CHUNK, HEADS_PER_MIXER * SGU_CHUNK)
        bsgu_tile = jnp.repeat(b_sgu[l].T, HEAD_DIM, axis=1)
        oa, oc, dq, dk, dv, bq, bk, bv = _mixer_in(
            x2d, g_mix_norm[l][None, :], w_in[l].astype(BF16), wsgu_cat, bsgu_tile, g_sgu_v[l][None, :],
            _tile_heads(g_q_dil[l]), _tile_heads(g_k_dil[l]), conv_w[l], _tile_heads(g_q_sb[l]),
            _tile_heads(g_k_sb[l]), bd, seq=s)
        r3 = lambda a: a.reshape(b, s, MIX_W)
        ob = _dilated(r3(dq), r3(dk), r3(dv), lc).reshape(b * s, MIX_W)
        od = _stickbreak(r3(bq), r3(bk), r3(bv), umat).reshape(b * s, MIX_W)
        x1, h2 = _mixer_out(x2d, oa, ob, oc, od, g_head_out[l][None, :], w_out[l].astype(BF16),
                            g_ffn_norm[l][None, :], bd)
        wqt = w_peer_q[l].T.astype(BF16)
        keys = peer_sub_keys[l].reshape(2 * PEER_HEADS, PEER_N_KEYS, -1).astype(BF16)
        r2, e2, n, e1 = _peer_route(h2, wqt, keys)
        x2d = _peer_ffn(h2, peer_u[l].astype(BF16), peer_v[l].T.astype(BF16), r2, e2, n, e1, x1)
    return x2d.reshape(b, s, d)
```

```python
import functools

import numpy as np
import jax
import jax.numpy as jnp
from jax import lax
from jax.experimental import pallas as pl
from jax.experimental.pallas import tpu as pltpu

HEAD_DIM = 64
MIX_W = 256
HEADS_PER_MIXER = MIX_W // HEAD_DIM
N_IN_PIECES = 11
SGU_CHUNK = 128
RMS_EPS = 1e-6
NEG = -1e30
LOG2E = 1.4426950408889634
PEER_HEADS = 8
PEER_N_KEYS = 128
PEER_TOPK = 16
ALIBI_SLOPES = tuple(2.0 ** (-8.0 * (i + 1) / HEADS_PER_MIXER) for i in range(HEADS_PER_MIXER))

LANES = 128
SUBLANES = 8
TM = 512
TQ = 256
DIL_MAX_DIST = 2048
DIL_NWIN = DIL_MAX_DIST // TQ + 1
DIL_WIN = DIL_NWIN * TQ
DIL_SHORT_TILES = 4
PEER_EC = 2048
ROUTE_BLOCKS_PER_TRIP = 8
PEER_GROUP = SUBLANES
VMEM_LIMIT = 56 * 1024 * 1024

F32 = jnp.float32
BF16 = jnp.bfloat16
_NT = (((1,), (1,)), ((), ()))


def _params(*sem):
    return pltpu.CompilerParams(dimension_semantics=sem, vmem_limit_bytes=VMEM_LIMIT)


def _head_sumsq(x2, bd):
    hi = x2.astype(BF16)
    lo = (x2 - hi.astype(F32)).astype(BF16)
    return (jnp.dot(hi, bd, preferred_element_type=F32) + jnp.dot(lo, bd, preferred_element_type=F32))


def _head_rms(x, g, bd):
    ss = _head_sumsq(x * x, bd)
    return x * lax.rsqrt(ss * (1.0 / HEAD_DIM) + RMS_EPS) * g


def _gelu_tanh(x):
    k = 2.0 * 0.7978845608028654 * LOG2E
    t = x * x * (-0.044715 * k) - k
    return x / (1.0 + jnp.exp2(x * t))


def _row_rms(x, g):
    ms = jnp.mean(x * x, axis=-1, keepdims=True)
    return x * lax.rsqrt(ms + RMS_EPS) * g


def _mixer_in_kernel(x_ref, gmix_ref, win_ref, wsgu_ref, bsgu_ref, gv_ref, gqd_ref, gkd_ref, convw_ref,
                     gqs_ref, gks_ref, bd_ref,
                     oa_ref, oc_ref, dq_ref, dk_ref, dv_ref, bq_ref, bk_ref, bv_ref,
                     ybuf_ref, *, tiles_per_seq):
    i = pl.program_id(0)
    tm = x_ref.shape[0]
    h = _row_rms(x_ref[...], gmix_ref[...]).astype(BF16)
    bd = bd_ref[...]
    scale = HEAD_DIM ** -0.5

    def piece(p):
        return jnp.dot(h, win_ref[:, p * MIX_W:(p + 1) * MIX_W], preferred_element_type=F32)

    u = jax.nn.gelu(piece(0))
    v = _head_rms(jax.nn.gelu(piece(1)), gv_ref[...], bd)
    t_idx = lax.broadcasted_iota(jnp.int32, (SGU_CHUNK, HEADS_PER_MIXER * SGU_CHUNK), 0)
    s_idx = lax.broadcasted_iota(jnp.int32, (SGU_CHUNK, HEADS_PER_MIXER * SGU_CHUNK), 1) & (SGU_CHUNK - 1)
    wcat = jnp.where(s_idx <= t_idx, wsgu_ref[...], 0.0).astype(BF16)
    lane_head = lax.broadcasted_iota(jnp.int32, (SGU_CHUNK, MIX_W), 1) >> 6
    bias = bsgu_ref[...]
    for c in range(tm // SGU_CHUNK):
        rows = slice(c * SGU_CHUNK, (c + 1) * SGU_CHUNK)
        vc = v[rows, :]
        vstack = jnp.concatenate(
            [jnp.where(lane_head == hh, vc, 0.0).astype(BF16) for hh in range(HEADS_PER_MIXER)], axis=0)
        mixed = jnp.dot(wcat, vstack, preferred_element_type=F32) + bias
        oa_ref[rows, :] = u[rows, :] * mixed

    dq_ref[...] = (_head_rms(piece(2), gqd_ref[...], bd) * scale).astype(BF16)
    dk_ref[...] = _head_rms(piece(3), gkd_ref[...], bd).astype(BF16)
    dv_ref[...] = piece(4).astype(BF16)
    bq_ref[...] = (_head_rms(piece(8), gqs_ref[...], bd) * scale).astype(BF16)
    bk_ref[...] = _head_rms(piece(9), gks_ref[...], bd).astype(BF16)
    bv_ref[...] = piece(10).astype(BF16)

    cb = piece(5)
    y = piece(6) * piece(7)

    @pl.when(i % tiles_per_seq == 0)
    def _():
        ybuf_ref[0:SUBLANES, :] = jnp.zeros((SUBLANES, MIX_W), F32)

    ybuf_ref[SUBLANES:SUBLANES + tm, :] = y
    y1 = ybuf_ref[SUBLANES - 1:SUBLANES - 1 + tm, :]
    y2 = ybuf_ref[SUBLANES - 2:SUBLANES - 2 + tm, :]
    w = convw_ref[...]
    oc_ref[...] = cb * (w[0:1, :] * y2 + w[1:2, :] * y1 + w[2:3, :] * y)
    ybuf_ref[0:SUBLANES, :] = y[tm - SUBLANES:tm, :]


def _mixer_in(x2d, gmix, win, wsgu_cat, bsgu_tile, gv, gqd, gkd, convw, gqs, gks, bd, *, seq):
    t = x2d.shape[0]
    d = x2d.shape[1]
    full = lambda a: pl.BlockSpec(a.shape, lambda i: (0,) * a.ndim)
    tok = lambda w: pl.BlockSpec((TM, w), lambda i: (i, 0))
    consts = (gmix, win, wsgu_cat, bsgu_tile, gv, gqd, gkd, convw, gqs, gks, bd)
    out_shape = ([jax.ShapeDtypeStruct((t, MIX_W), F32)] * 2 + [jax.ShapeDtypeStruct((t, MIX_W), BF16)] * 6)
    return pl.pallas_call(
        functools.partial(_mixer_in_kernel, tiles_per_seq=seq // TM),
        grid=(t // TM,),
        in_specs=[tok(d)] + [full(a) for a in consts],
        out_specs=[tok(MIX_W)] * 8,
        out_shape=out_shape,
        scratch_shapes=[pltpu.VMEM((TM + SUBLANES, MIX_W), F32)],
        compiler_params=_params("arbitrary"),
    )(x2d, *consts)


def _dilated_logcount_table():
    n = 2 * DIL_NWIN - 1
    qi = np.arange(TQ)[:, None]
    kj = np.arange(TQ)[None, :]
    tiles = []
    for u in range(n):
        off = DIL_NWIN - 1 - u
        dd = off * TQ + qi - kj
        cnt = ((dd >= 0) & (dd <= 128)).astype(np.int64)
        cnt += ((dd >= 0) & (dd <= 512) & (dd % 4 == 0))
        cnt += ((dd >= 0) & (dd <= 2048) & (dd % 16 == 0))
        tiles.append(np.where(cnt > 0, np.log(np.maximum(cnt, 1).astype(np.float64)), NEG))
    return np.concatenate(tiles, axis=1).astype(np.float32)


def _dilated_kernel(q_ref, k_ref, v_ref, lc_ref, o_ref):
    qb = pl.program_id(1)
    o0 = jnp.minimum(qb, DIL_NWIN - 1)
    kstart = pl.multiple_of((qb - o0) * TQ, TQ)
    cstart = pl.multiple_of((DIL_NWIN - 1 - o0) * TQ, TQ)
    lane_head = lax.broadcasted_iota(jnp.int32, (TQ, LANES), 1) >> 6

    def attend(win):
        lcw = lc_ref[:, pl.ds(cstart, win)]
        qi = lax.broadcasted_iota(jnp.int32, (TQ, win), 0)
        col = lax.broadcasted_iota(jnp.int32, (TQ, win), 1)
        dist = (o0 * TQ + qi - col).astype(F32)
        q = q_ref[0]
        kw = k_ref[0, pl.ds(kstart, win), :]
        vw = v_ref[0, pl.ds(kstart, win), :]
        for pair in range(2):
            cols = slice(pair * LANES, (pair + 1) * LANES)
            qp, kp, vp = q[:, cols], kw[:, cols], vw[:, cols]
            outs = []
            for hh in range(2):
                qm = jnp.where(lane_head == hh, qp, jnp.zeros_like(qp))
                s = lax.dot_general(qm, kp, _NT, preferred_element_type=F32)
                s = s + (lcw - ALIBI_SLOPES[2 * pair + hh] * dist)
                m = jnp.max(s, axis=-1, keepdims=True)
                p = jnp.exp(s - m)
                l = jnp.sum(p, axis=-1, keepdims=True)
                o = jnp.dot(p.astype(BF16), vp, preferred_element_type=F32)
                outs.append(o / l)
            o_ref[0, :, cols] = jnp.where(lane_head == 0, outs[0], outs[1])

    @pl.when(qb < DIL_SHORT_TILES)
    def _():
        attend(DIL_SHORT_TILES * TQ)

    @pl.when(qb >= DIL_SHORT_TILES)
    def _():
        attend(DIL_WIN)


def _dilated(q, k, v, lc):
    b, s, w = q.shape
    return pl.pallas_call(
        _dilated_kernel,
        grid=(b, s // TQ),
        in_specs=[pl.BlockSpec((1, TQ, w), lambda bi, qi: (bi, qi, 0)),
                  pl.BlockSpec((1, s, w), lambda bi, qi: (bi, 0, 0)),
                  pl.BlockSpec((1, s, w), lambda bi, qi: (bi, 0, 0)),
                  pl.BlockSpec(lc.shape, lambda bi, qi: (0, 0))],
        out_specs=pl.BlockSpec((1, TQ, w), lambda bi, qi: (bi, qi, 0)),
        out_shape=jax.ShapeDtypeStruct((b, s, w), F32),
        compiler_params=_params("arbitrary", "arbitrary"),
    )(q, k, v, lc)


def _suffix_sum_matrix():
    j = np.arange(TQ)[:, None]
    s = np.arange(TQ)[None, :]
    return (j >= s).astype(np.float32)


def _stickbreak_kernel(q_ref, k_ref, v_ref, u_ref, o_ref, acc_ref, run_ref):
    qb = pl.program_id(1)
    q = q_ref[0]
    umat = u_ref[...]
    acc_ref[...] = jnp.zeros(acc_ref.shape, F32)
    run_ref[...] = jnp.zeros(run_ref.shape, F32)
    lane_head = lax.broadcasted_iota(jnp.int32, (TQ, LANES), 1) >> 6
    qstack = []
    for pair in range(2):
        qp = q[:, pair * LANES:(pair + 1) * LANES]
        qstack.append(jnp.concatenate(
            [jnp.where(lane_head == hh, qp, jnp.zeros_like(qp)) for hh in range(2)], axis=0))
    nh = HEADS_PER_MIXER
    row = lax.broadcasted_iota(jnp.int32, (nh * TQ, TQ), 0) & (TQ - 1)
    colk = lax.broadcasted_iota(jnp.int32, (nh * TQ, TQ), 1)
    causal = colk < row

    def scores(kb, diagonal):
        kt = k_ref[0, pl.ds(pl.multiple_of(kb * TQ, TQ), TQ), :]
        y = jnp.concatenate(
            [lax.dot_general(qstack[pair], kt[:, pair * LANES:(pair + 1) * LANES], _NT, preferred_element_type=F32)
             for pair in range(2)], axis=0) * LOG2E
        nl = jnp.maximum(y, 0.0) + jnp.log2(1.0 + jnp.exp2(-jnp.abs(y)))
        if diagonal:
            nl = jnp.where(causal, nl, 0.0)
        cs = jnp.dot(nl.astype(BF16), umat, preferred_element_type=F32)
        return y, cs

    def values(kb, diagonal, y, cs):
        vt = v_ref[0, pl.ds(pl.multiple_of(kb * TQ, TQ), TQ), :]
        run = run_ref[...]
        a = jnp.exp2(y - (cs + run))
        if diagonal:
            a = jnp.where(causal, a, 0.0)
        a = a.astype(BF16)
        for pair in range(2):
            cols = slice(pair * LANES, (pair + 1) * LANES)
            pv = jnp.dot(a[2 * pair * TQ:(2 * pair + 2) * TQ, :], vt[:, cols], preferred_element_type=F32)
            acc_ref[:, cols] += jnp.where(lane_head == 0, pv[:TQ, :], pv[TQ:, :])
        run_ref[...] = run + cs[:, 0:1]

    values(qb, True, *scores(qb, True))

    @pl.when(qb % 2 == 1)
    def _():
        values(qb - 1, False, *scores(qb - 1, False))

    def body(j, carry):
        k1 = qb - (qb % 2) - 1 - 2 * j
        s1 = scores(k1, False)
        s2 = scores(k1 - 1, False)
        values(k1, False, *s1)
        values(k1 - 1, False, *s2)
        return carry

    lax.fori_loop(0, qb // 2, body, 0)
    o_ref[0] = acc_ref[...]


def _stickbreak(q, k, v, umat):
    b, s, w = q.shape
    return pl.pallas_call(
        _stickbreak_kernel,
        grid=(b, s // TQ),
        in_specs=[pl.BlockSpec((1, TQ, w), lambda bi, qi: (bi, qi, 0)),
                  pl.BlockSpec((1, s, w), lambda bi, qi: (bi, 0, 0)),
                  pl.BlockSpec((1, s, w), lambda bi, qi: (bi, 0, 0)),
                  pl.BlockSpec(umat.shape, lambda bi, qi: (0, 0))],
        out_specs=pl.BlockSpec((1, TQ, w), lambda bi, qi: (bi, qi, 0)),
        out_shape=jax.ShapeDtypeStruct((b, s, w), F32),
        scratch_shapes=[pltpu.VMEM((TQ, w), F32), pltpu.VMEM((HEADS_PER_MIXER * TQ, 1), F32)],
        compiler_params=_params("arbitrary", "arbitrary"),
    )(q, k, v, umat)


def _mixer_out_kernel(x_ref, oa_ref, ob_ref, oc_ref, od_ref, gho_ref, wout_ref, gffn_ref, bd_ref,
                      x1_ref, h2_ref):
    bd = bd_ref[...]
    acc = x_ref[...]
    for p, r in enumerate((oa_ref, ob_ref, oc_ref, od_ref)):
        cols = slice(p * MIX_W, (p + 1) * MIX_W)
        y = _head_rms(r[...], gho_ref[:, cols], bd).astype(BF16)
        acc = acc + jnp.dot(y, wout_ref[cols, :], preferred_element_type=F32)
    x1_ref[...] = acc
    h2_ref[...] = _row_rms(acc, gffn_ref[...]).astype(BF16)


def _mixer_out(x2d, oa, ob, oc, od, gho, wout, gffn, bd):
    t, d = x2d.shape
    full = lambda a: pl.BlockSpec(a.shape, lambda i: (0,) * a.ndim)
    tok = lambda w: pl.BlockSpec((TM, w), lambda i: (i, 0))
    return pl.pallas_call(
        _mixer_out_kernel,
        grid=(t // TM,),
        in_specs=[tok(d)] + [tok(MIX_W)] * 4 + [full(gho), full(wout), full(gffn), full(bd)],
        out_specs=[tok(d), tok(d)],
        out_shape=[jax.ShapeDtypeStruct((t, d), F32), jax.ShapeDtypeStruct((t, d), BF16)],
        compiler_params=_params("arbitrary"),
    )(x2d, oa, ob, oc, od, gho, wout, gffn, bd)


def _top16_rows(s):
    kidx = lax.broadcasted_iota(jnp.int32, s.shape, 0).astype(F32)
    rank = jnp.full(s.shape, float(PEER_TOPK), F32)
    vals = []
    for r in range(PEER_TOPK):
        m = jnp.max(s, axis=0, keepdims=True)
        first = jnp.min(jnp.where(s == m, kidx, float(PEER_N_KEYS)), axis=0, keepdims=True)
        hit = kidx == first
        rank = jnp.where(hit, float(r), rank)
        s = jnp.where(hit, -jnp.inf, s)
        vals.append(m)
    return vals, rank


def _top16_rows_unique(s, want_rank):
    rank = jnp.full(s.shape, float(PEER_TOPK), F32) if want_rank else None
    vals = []
    for r in range(PEER_TOPK):
        m = jnp.max(s, axis=0, keepdims=True)
        hit = s == m
        if want_rank:
            rank = jnp.where(hit, float(r), rank)
        s = jnp.where(hit, -jnp.inf, s)
        vals.append(m)
    hits = jnp.sum(jnp.where(s == -jnp.inf, 1.0, 0.0), axis=0, keepdims=True)
    return vals, rank, hits


def _peer_route_kernel(h2_ref, wqt_ref, keys_ref, r2_ref, e2_ref, n_ref, e1_ref, st_ref):
    tm = h2_ref.shape[0]
    qt = lax.dot_general(wqt_ref[...], h2_ref[...], _NT, preferred_element_type=F32).astype(BF16)
    for hp in range(2 * PEER_HEADS):
        rows = slice(hp * PEER_N_KEYS, (hp + 1) * PEER_N_KEYS)
        st_ref[hp] = jnp.dot(keys_ref[hp], qt[rows, :], preferred_element_type=F32)

    nlb = tm // LANES
    sub = lax.broadcasted_iota(jnp.int32, (8, LANES), 0).astype(F32)
    pos = jnp.concatenate([sub, sub + 8.0] + [sub + 16.0 * a for a in range(1, 8)] + [(sub + 8.0) * 16.0], axis=0)
    sub16 = lax.broadcasted_iota(jnp.int32, (PEER_TOPK, LANES), 0).astype(F32)

    def candidates(v1, v2):
        sv1 = jnp.zeros((PEER_TOPK, LANES), F32)
        sv2 = jnp.zeros((PEER_TOPK, LANES), F32)
        for r in range(PEER_TOPK):
            sv1 = jnp.where(sub16 == float(r), v1[r], sv1)
            sv2 = jnp.where(sub16 == float(r), v2[r], sv2)
        return jnp.concatenate(
            [v1[0] + sv2[0:8], v1[0] + sv2[8:16]] + [v1[a] + sv2[0:8] for a in range(1, 8)] + [sv1[8:16] + v2[0]],
            axis=0)

    def store(hd, lb, s1, s2, v1, v2, rk2, cnt, zsum):
        r2_ref[hd, lb] = rk2.astype(BF16)
        e2_ref[hd, lb] = jnp.exp(s2 - v2[0]).astype(BF16)
        n_ref[hd, lb] = cnt
        e1_ref[hd, lb] = jnp.exp(s1 - v1[0]) * (1.0 / zsum)

    def fast(hd, lb):
        lo = pl.multiple_of(lb * LANES, LANES)
        s1 = st_ref[2 * hd, :, pl.ds(lo, LANES)]
        s2 = st_ref[2 * hd + 1, :, pl.ds(lo, LANES)]

        v1, _, hits1 = _top16_rows_unique(s1, False)
        v2, rk2, hits2 = _top16_rows_unique(s2, True)
        cand = candidates(v1, v2)
        zsum = jnp.zeros((1, LANES), F32)
        top0 = v1[0] + v2[0]
        for r in range(PEER_TOPK):
            m = jnp.max(cand, axis=0, keepdims=True)
            cand = jnp.where(cand == m, -jnp.inf, cand)
            zsum = zsum + jnp.exp(m - top0)
        sel = jnp.where(cand == -jnp.inf, 1.0, 0.0)
        na = [jnp.sum(sel[0:16], axis=0, keepdims=True)]
        na += [jnp.sum(sel[8 + 8 * a:16 + 8 * a], axis=0, keepdims=True) for a in range(1, 8)]
        last = sel[72:80]
        cnt = jnp.zeros((PEER_N_KEYS, LANES), F32)
        for a in range(8):
            cnt = jnp.where(s1 == v1[a], na[a], cnt)
        for a in range(8, PEER_TOPK):
            cnt = jnp.where(s1 == v1[a], last[a - 8:a - 7], cnt)
        store(hd, lb, s1, s2, v1, v2, rk2, cnt, zsum)
        picked = jnp.sum(sel, axis=0, keepdims=True)
        return jnp.max(jnp.maximum(jnp.maximum(hits1, hits2), picked))

    def exact(hd, lb):
        lo = pl.multiple_of(lb * LANES, LANES)
        s1 = st_ref[2 * hd, :, pl.ds(lo, LANES)]
        s2 = st_ref[2 * hd + 1, :, pl.ds(lo, LANES)]
        v1, rk1 = _top16_rows(s1)
        v2, rk2 = _top16_rows(s2)
        cand = candidates(v1, v2)
        cnt = jnp.zeros((PEER_N_KEYS, LANES), F32)
        zsum = jnp.zeros((1, LANES), F32)
        top0 = v1[0] + v2[0]
        for r in range(PEER_TOPK):
            m = jnp.max(cand, axis=0, keepdims=True)
            pr = jnp.min(jnp.where(cand == m, pos, 4096.0), axis=0, keepdims=True)
            cand = jnp.where(pos == pr, -jnp.inf, cand)
            cnt = cnt + jnp.where(rk1 == jnp.floor(pr * (1.0 / 16.0)), 1.0, 0.0)
            zsum = zsum + jnp.exp(m - top0)
        store(hd, lb, s1, s2, v1, v2, rk2, cnt, zsum)

    def body(it, carry):
        blocks = [((it * ROUTE_BLOCKS_PER_TRIP + k) // nlb, (it * ROUTE_BLOCKS_PER_TRIP + k) % nlb)
                  for k in range(ROUTE_BLOCKS_PER_TRIP)]
        worst = [fast(hd, lb) for hd, lb in blocks]
        for (hd, lb), w in zip(blocks, worst):
            pl.when(w > PEER_TOPK + 0.5)(functools.partial(exact, hd, lb))
        return carry

    lax.fori_loop(0, PEER_HEADS * nlb // ROUTE_BLOCKS_PER_TRIP, body, 0)


def _peer_route(h2, wqt, keys):
    t, d = h2.shape
    tabs = lambda dt: jax.ShapeDtypeStruct((PEER_HEADS, t // LANES, PEER_N_KEYS, LANES), dt)
    tab_spec = pl.BlockSpec((PEER_HEADS, TM // LANES, PEER_N_KEYS, LANES), lambda i: (0, i, 0, 0))
    return pl.pallas_call(
        _peer_route_kernel,
        grid=(t // TM,),
        in_specs=[pl.BlockSpec((TM, d), lambda i: (i, 0)),
                  pl.BlockSpec(wqt.shape, lambda i: (0, 0)),
                  pl.BlockSpec(keys.shape, lambda i: (0, 0, 0))],
        out_specs=[tab_spec] * 4,
        out_shape=[tabs(BF16), tabs(BF16), tabs(F32), tabs(F32)],
        scratch_shapes=[pltpu.VMEM((2 * PEER_HEADS, PEER_N_KEYS, TM), F32)],
        compiler_params=_params("arbitrary"),
    )(h2, wqt, keys)


def _peer_ffn_kernel(h2_ref, u_ref, vt_ref, r2_ref, e2_ref, n_ref, e1_ref, x1_ref, o_ref,
                     acc_ref, at_ref, wt_ref):
    c = pl.program_id(1)
    tm = h2_ref.shape[0]
    ec = u_ref.shape[0]
    ngroup = ec // (PEER_GROUP * PEER_N_KEYS)

    @pl.when(c == 0)
    def _():
        acc_ref[...] = jnp.zeros(acc_ref.shape, F32)

    at_ref[...] = lax.dot_general(u_ref[...], h2_ref[...], _NT, preferred_element_type=F32)

    def group(gi, carry):
        grp = pl.ds(pl.multiple_of((c * ngroup + gi) * PEER_GROUP, PEER_GROUP), PEER_GROUP)
        for k in range(PEER_GROUP):
            rows = pl.ds(pl.multiple_of((gi * PEER_GROUP + k) * PEER_N_KEYS, PEER_N_KEYS), PEER_N_KEYS)
            for lb in range(tm // LANES):
                cols = slice(lb * LANES, (lb + 1) * LANES)
                gw = jnp.zeros((PEER_N_KEYS, LANES), BF16)
                for hd in range(PEER_HEADS):
                    nrow = n_ref[hd, lb, grp, :].astype(BF16)[k:k + 1, :]
                    e1row = e1_ref[hd, lb, grp, :].astype(BF16)[k:k + 1, :]
                    margin = nrow - r2_ref[hd, lb]
                    gw = gw + jnp.maximum(jnp.minimum(e1row * e2_ref[hd, lb], margin), 0.0)
                wt_ref[rows, cols] = _gelu_tanh(at_ref[rows, cols]).astype(BF16) * gw
        return carry

    lax.fori_loop(0, ngroup, group, 0)
    acc_ref[...] += jnp.dot(vt_ref[...], wt_ref[...], preferred_element_type=F32)

    @pl.when(c == pl.num_programs(1) - 1)
    def _():
        o_ref[...] = x1_ref[...] + acc_ref[...].T


def _peer_ffn(h2, u, vt, r2, e2, n, e1, x1):
    t, d = h2.shape
    ne = u.shape[0]
    tab_spec = pl.BlockSpec((PEER_HEADS, TM // LANES, PEER_N_KEYS, LANES), lambda i, c: (0, i, 0, 0))
    return pl.pallas_call(
        _peer_ffn_kernel,
        grid=(t // TM, ne // PEER_EC),
        in_specs=[pl.BlockSpec((TM, d), lambda i, c: (i, 0)),
                  pl.BlockSpec((PEER_EC, d), lambda i, c: (c, 0)),
                  pl.BlockSpec((d, PEER_EC), lambda i, c: (0, c)),
                  tab_spec, tab_spec, tab_spec, tab_spec,
                  pl.BlockSpec((TM, d), lambda i, c: (i, 0))],
        out_specs=pl.BlockSpec((TM, d), lambda i, c: (i, 0)),
        out_shape=jax.ShapeDtypeStruct((t, d), F32),
        scratch_shapes=[pltpu.VMEM((d, TM), F32), pltpu.VMEM((PEER_EC, TM), F32), pltpu.VMEM((PEER_EC, TM), BF16)],
        compiler_params=_params("arbitrary", "arbitrary"),
    )(h2, u, vt, r2, e2, n, e1, x1)


def _tile_heads(g):
    return jnp.tile(g, HEADS_PER_MIXER)[None, :]


def kernel(x, g_mix_norm, w_in, w_sgu, b_sgu, g_sgu_v, g_q_dil, g_k_dil, conv_w, g_q_sb, g_k_sb, g_head_out,
           w_out, g_ffn_norm, w_peer_q, peer_sub_keys, peer_u, peer_v):
    b, s, d = x.shape
    depth = w_in.shape[0]
    assert d == 4 * MIX_W and s % TM == 0 and s >= DIL_WIN
    assert w_in.shape[2] == N_IN_PIECES * MIX_W

    hh = np.arange(MIX_W) // HEAD_DIM
    bd = jnp.asarray(hh[:, None] == hh[None, :], BF16)
    lc = jnp.asarray(_dilated_logcount_table())
    umat = jnp.asarray(_suffix_sum_matrix(), BF16)

    x2d = x.reshape(b * s, d)
    for l in range(depth):
        wsgu_cat = w_sgu[l].transpose(1, 0, 2).reshape(SGU_CHUNK, HEADS_PER_MIXER * SGU_CHUNK)
        bsgu_tile = jnp.repeat(b_sgu[l].T, HEAD_DIM, axis=1)
        oa, oc, dq, dk, dv, bq, bk, bv = _mixer_in(
            x2d, g_mix_norm[l][None, :], w_in[l].astype(BF16), wsgu_cat, bsgu_tile, g_sgu_v[l][None, :],
            _tile_heads(g_q_dil[l]), _tile_heads(g_k_dil[l]), conv_w[l], _tile_heads(g_q_sb[l]),
            _tile_heads(g_k_sb[l]), bd, seq=s)
        r3 = lambda a: a.reshape(b, s, MIX_W)
        ob = _dilated(r3(dq), r3(dk), r3(dv), lc).reshape(b * s, MIX_W)
        od = _stickbreak(r3(bq), r3(bk), r3(bv), umat).reshape(b * s, MIX_W)
        x1, h2 = _mixer_out(x2d, oa, ob, oc, od, g_head_out[l][None, :], w_out[l].astype(BF16),
                            g_ffn_norm[l][None, :], bd)
        wqt = w_peer_q[l].T.astype(BF16)
        keys = peer_sub_keys[l].reshape(2 * PEER_HEADS, PEER_N_KEYS, -1).astype(BF16)
        r2, e2, n, e1 = _peer_route(h2, wqt, keys)
        x2d = _peer_ffn(h2, peer_u[l].astype(BF16), peer_v[l].T.astype(BF16), r2, e2, n, e1, x1)
    return x2d.reshape(b, s, d)
```

```python
import functools

import numpy as np
import jax
import jax.numpy as jnp
from jax import lax
from jax.experimental import pallas as pl
from jax.experimental.pallas import tpu as pltpu

HEAD_DIM = 64
MIX_W = 256
HEADS_PER_MIXER = MIX_W // HEAD_DIM
N_IN_PIECES = 11
SGU_CHUNK = 128
RMS_EPS = 1e-6
NEG = -1e30
LOG2E = 1.4426950408889634
PEER_HEADS = 8
PEER_N_KEYS = 128
PEER_TOPK = 16
ALIBI_SLOPES = tuple(2.0 ** (-8.0 * (i + 1) / HEADS_PER_MIXER) for i in range(HEADS_PER_MIXER))

LANES = 128
SUBLANES = 8
TM = 512
TQ = 256
DIL_MAX_DIST = 2048
DIL_NWIN = DIL_MAX_DIST // TQ + 1
DIL_WIN = DIL_NWIN * TQ
DIL_SHORT_TILES = 4
PEER_EC = 2048
ROUTE_BLOCKS_PER_TRIP = 8
PEER_GROUP = SUBLANES
VMEM_LIMIT = 56 * 1024 * 1024

F32 = jnp.float32
BF16 = jnp.bfloat16
_NT = (((1,), (1,)), ((), ()))


def _params(*sem):
    return pltpu.CompilerParams(dimension_semantics=sem, vmem_limit_bytes=VMEM_LIMIT)


def _head_sumsq(x2, bd):
    hi = x2.astype(BF16)
    lo = (x2 - hi.astype(F32)).astype(BF16)
    return (jnp.dot(hi, bd, preferred_element_type=F32) + jnp.dot(lo, bd, preferred_element_type=F32))


def _head_rms(x, g, bd):
    ss = _head_sumsq(x * x, bd)
    return x * lax.rsqrt(ss * (1.0 / HEAD_DIM) + RMS_EPS) * g


def _gelu_tanh(x):
    k = 2.0 * 0.7978845608028654 * LOG2E
    t = x * x * (-0.044715 * k) - k
    return x / (1.0 + jnp.exp2(x * t))


def _row_rms(x, g):
    ms = jnp.mean(x * x, axis=-1, keepdims=True)
    return x * lax.rsqrt(ms + RMS_EPS) * g


def _mixer_in_kernel(x_ref, gmix_ref, win_ref, wsgu_ref, bsgu_ref, gv_ref, gqd_ref, gkd_ref, convw_ref,
                     gqs_ref, gks_ref, bd_ref,
                     oa_ref, oc_ref, dq_ref, dk_ref, dv_ref, bq_ref, bk_ref, bv_ref,
                     ybuf_ref, *, tiles_per_seq):
    i = pl.program_id(0)
    tm = x_ref.shape[0]
    h = _row_rms(x_ref[...], gmix_ref[...]).astype(BF16)
    bd = bd_ref[...]
    scale = HEAD_DIM ** -0.5

    def piece(p):
        return jnp.dot(h, win_ref[:, p * MIX_W:(p + 1) * MIX_W], preferred_element_type=F32)

    u = jax.nn.gelu(piece(0))
    v = _head_rms(jax.nn.gelu(piece(1)), gv_ref[...], bd)
    t_idx = lax.broadcasted_iota(jnp.int32, (SGU_CHUNK, HEADS_PER_MIXER * SGU_CHUNK), 0)
    s_idx = lax.broadcasted_iota(jnp.int32, (SGU_CHUNK, HEADS_PER_MIXER * SGU_CHUNK), 1) & (SGU_CHUNK - 1)
    wcat = jnp.where(s_idx <= t_idx, wsgu_ref[...], 0.0).astype(BF16)
    lane_head = lax.broadcasted_iota(jnp.int32, (SGU_CHUNK, MIX_W), 1) >> 6
    bias = bsgu_ref[...]
    for c in range(tm // SGU_CHUNK):
        rows = slice(c * SGU_CHUNK, (c + 1) * SGU_CHUNK)
        vc = v[rows, :]
        vstack = jnp.concatenate(
            [jnp.where(lane_head == hh, vc, 0.0).astype(BF16) for hh in range(HEADS_PER_MIXER)], axis=0)
        mixed = jnp.dot(wcat, vstack, preferred_element_type=F32) + bias
        oa_ref[rows, :] = u[rows, :] * mixed

    dq_ref[...] = (_head_rms(piece(2), gqd_ref[...], bd) * scale).astype(BF16)
    dk_ref[...] = _head_rms(piece(3), gkd_ref[...], bd).astype(BF16)
    dv_ref[...] = piece(4).astype(BF16)
    bq_ref[...] = (_head_rms(piece(8), gqs_ref[...], bd) * scale).astype(BF16)
    bk_ref[...] = _head_rms(piece(9), gks_ref[...], bd).astype(BF16)
    bv_ref[...] = piece(10).astype(BF16)

    cb = piece(5)
    y = piece(6) * piece(7)

    @pl.when(i % tiles_per_seq == 0)
    def _():
        ybuf_ref[0:SUBLANES, :] = jnp.zeros((SUBLANES, MIX_W), F32)

    ybuf_ref[SUBLANES:SUBLANES + tm, :] = y
    y1 = ybuf_ref[SUBLANES - 1:SUBLANES - 1 + tm, :]
    y2 = ybuf_ref[SUBLANES - 2:SUBLANES - 2 + tm, :]
    w = convw_ref[...]
    oc_ref[...] = cb * (w[0:1, :] * y2 + w[1:2, :] * y1 + w[2:3, :] * y)
    ybuf_ref[0:SUBLANES, :] = y[tm - SUBLANES:tm, :]


def _mixer_in(x2d, gmix, win, wsgu_cat, bsgu_tile, gv, gqd, gkd, convw, gqs, gks, bd, *, seq):
    t = x2d.shape[0]
    d = x2d.shape[1]
    full = lambda a: pl.BlockSpec(a.shape, lambda i: (0,) * a.ndim)
    tok = lambda w: pl.BlockSpec((TM, w), lambda i: (i, 0))
    consts = (gmix, win, wsgu_cat, bsgu_tile, gv, gqd, gkd, convw, gqs, gks, bd)
    out_shape = ([jax.ShapeDtypeStruct((t, MIX_W), F32)] * 2 + [jax.ShapeDtypeStruct((t, MIX_W), BF16)] * 6)
    return pl.pallas_call(
        functools.partial(_mixer_in_kernel, tiles_per_seq=seq // TM),
        grid=(t // TM,),
        in_specs=[tok(d)] + [full(a) for a in consts],
        out_specs=[tok(MIX_W)] * 8,
        out_shape=out_shape,
        scratch_shapes=[pltpu.VMEM((TM + SUBLANES, MIX_W), F32)],
        compiler_params=_params("arbitrary"),
    )(x2d, *consts)


def _dilated_logcount_table():
    n = 2 * DIL_NWIN - 1
    qi = np.arange(TQ)[:, None]
    kj = np.arange(TQ)[None, :]
    tiles = []
    for u in range(n):
        off = DIL_NWIN - 1 - u
        dd = off * TQ + qi - kj
        cnt = ((dd >= 0) & (dd <= 128)).astype(np.int64)
        cnt += ((dd >= 0) & (dd <= 512) & (dd % 4 == 0))
        cnt += ((dd >= 0) & (dd <= 2048) & (dd % 16 == 0))
        tiles.append(np.where(cnt > 0, np.log(np.maximum(cnt, 1).astype(np.float64)), NEG))
    return np.concatenate(tiles, axis=1).astype(np.float32)


def _dilated_kernel(q_ref, k_ref, v_ref, lc_ref, o_ref):
    qb = pl.program_id(1)
    o0 = jnp.minimum(qb, DIL_NWIN - 1)
    kstart = pl.multiple_of((qb - o0) * TQ, TQ)
    cstart = pl.multiple_of((DIL_NWIN - 1 - o0) * TQ, TQ)
    lane_head = lax.broadcasted_iota(jnp.int32, (TQ, LANES), 1) >> 6

    def attend(win):
        lcw = lc_ref[:, pl.ds(cstart, win)]
        qi = lax.broadcasted_iota(jnp.int32, (TQ, win), 0)
        col = lax.broadcasted_iota(jnp.int32, (TQ, win), 1)
        dist = (o0 * TQ + qi - col).astype(F32)
        q = q_ref[0]
        kw = k_ref[0, pl.ds(kstart, win), :]
        vw = v_ref[0, pl.ds(kstart, win), :]
        for pair in range(2):
            cols = slice(pair * LANES, (pair + 1) * LANES)
            qp, kp, vp = q[:, cols], kw[:, cols], vw[:, cols]
            qs = jnp.concatenate([jnp.where(lane_head == hh, qp, jnp.zeros_like(qp)) for hh in range(2)], axis=0)
            bias = jnp.concatenate([lcw - ALIBI_SLOPES[2 * pair + hh] * dist for hh in range(2)], axis=0)
            s = lax.dot_general(qs, kp, _NT, preferred_element_type=F32) + bias
            m = jnp.max(s, axis=-1, keepdims=True)
            p = jnp.exp(s - m)
            l = jnp.sum(p, axis=-1, keepdims=True)
            o = jnp.dot(p.astype(BF16), vp, preferred_element_type=F32) / l
            o_ref[0, :, cols] = jnp.where(lane_head == 0, o[:TQ, :], o[TQ:, :])

    @pl.when(qb < DIL_SHORT_TILES)
    def _():
        attend(DIL_SHORT_TILES * TQ)

    @pl.when(qb >= DIL_SHORT_TILES)
    def _():
        attend(DIL_WIN)


def _dilated(q, k, v, lc):
    b, s, w = q.shape
    return pl.pallas_call(
        _dilated_kernel,
        grid=(b, s // TQ),
        in_specs=[pl.BlockSpec((1, TQ, w), lambda bi, qi: (bi, qi, 0)),
                  pl.BlockSpec((1, s, w), lambda bi, qi: (bi, 0, 0)),
                  pl.BlockSpec((1, s, w), lambda bi, qi: (bi, 0, 0)),
                  pl.BlockSpec(lc.shape, lambda bi, qi: (0, 0))],
        out_specs=pl.BlockSpec((1, TQ, w), lambda bi, qi: (bi, qi, 0)),
        out_shape=jax.ShapeDtypeStruct((b, s, w), F32),
        compiler_params=_params("arbitrary", "arbitrary"),
    )(q, k, v, lc)


def _suffix_sum_matrix():
    j = np.arange(TQ)[:, None]
    s = np.arange(TQ)[None, :]
    return (j >= s).astype(np.float32)


def _stickbreak_kernel(q_ref, k_ref, v_ref, u_ref, o_ref, acc_ref, run_ref):
    qb = pl.program_id(1)
    q = q_ref[0]
    umat = u_ref[...]
    acc_ref[...] = jnp.zeros(acc_ref.shape, F32)
    run_ref[...] = jnp.zeros(run_ref.shape, F32)
    lane_head = lax.broadcasted_iota(jnp.int32, (TQ, LANES), 1) >> 6
    qstack = []
    for pair in range(2):
        qp = q[:, pair * LANES:(pair + 1) * LANES]
        qstack.append(jnp.concatenate(
            [jnp.where(lane_head == hh, qp, jnp.zeros_like(qp)) for hh in range(2)], axis=0))
    nh = HEADS_PER_MIXER
    row = lax.broadcasted_iota(jnp.int32, (nh * TQ, TQ), 0) & (TQ - 1)
    colk = lax.broadcasted_iota(jnp.int32, (nh * TQ, TQ), 1)
    causal = colk < row

    def scores(kb, diagonal):
        kt = k_ref[0, pl.ds(pl.multiple_of(kb * TQ, TQ), TQ), :]
        y = jnp.concatenate(
            [lax.dot_general(qstack[pair], kt[:, pair * LANES:(pair + 1) * LANES], _NT, preferred_element_type=F32)
             for pair in range(2)], axis=0) * LOG2E
        nl = jnp.maximum(y, 0.0) + jnp.log2(1.0 + jnp.exp2(-jnp.abs(y)))
        if diagonal:
            nl = jnp.where(causal, nl, 0.0)
        cs = jnp.dot(nl.astype(BF16), umat, preferred_element_type=F32)
        return y, cs

    def values(kb, diagonal, y, cs):
        vt = v_ref[0, pl.ds(pl.multiple_of(kb * TQ, TQ), TQ), :]
        run = run_ref[...]
        a = jnp.exp2(y - (cs + run))
        if diagonal:
            a = jnp.where(causal, a, 0.0)
        a = a.astype(BF16)
        for pair in range(2):
            cols = slice(pair * LANES, (pair + 1) * LANES)
            pv = jnp.dot(a[2 * pair * TQ:(2 * pair + 2) * TQ, :], vt[:, cols], preferred_element_type=F32)
            acc_ref[:, cols] += jnp.where(lane_head == 0, pv[:TQ, :], pv[TQ:, :])
        run_ref[...] = run + cs[:, 0:1]

    values(qb, True, *scores(qb, True))

    @pl.when(qb % 2 == 1)
    def _():
        values(qb - 1, False, *scores(qb - 1, False))

    def body(j, carry):
        k1 = qb - (qb % 2) - 1 - 2 * j
        s1 = scores(k1, False)
        s2 = scores(k1 - 1, False)
        values(k1, False, *s1)
        values(k1 - 1, False, *s2)
        return carry

    lax.fori_loop(0, qb // 2, body, 0)
    o_ref[0] = acc_ref[...]


def _stickbreak(q, k, v, umat):
    b, s, w = q.shape
    return pl.pallas_call(
        _stickbreak_kernel,
        grid=(b, s // TQ),
        in_specs=[pl.BlockSpec((1, TQ, w), lambda bi, qi: (bi, qi, 0)),
                  pl.BlockSpec((1, s, w), lambda bi, qi: (bi, 0, 0)),
                  pl.BlockSpec((1, s, w), lambda bi, qi: (bi, 0, 0)),
                  pl.BlockSpec(umat.shape, lambda bi, qi: (0, 0))],
        out_specs=pl.BlockSpec((1, TQ, w), lambda bi, qi: (bi, qi, 0)),
        out_shape=jax.ShapeDtypeStruct((b, s, w), F32),
        scratch_shapes=[pltpu.VMEM((TQ, w), F32), pltpu.VMEM((HEADS_PER_MIXER * TQ, 1), F32)],
        compiler_params=_params("arbitrary", "arbitrary"),
    )(q, k, v, umat)


def _mixer_out_kernel(x_ref, oa_ref, ob_ref, oc_ref, od_ref, gho_ref, wout_ref, gffn_ref, bd_ref,
                      x1_ref, h2_ref):
    bd = bd_ref[...]
    acc = x_ref[...]
    for p, r in enumerate((oa_ref, ob_ref, oc_ref, od_ref)):
        cols = slice(p * MIX_W, (p + 1) * MIX_W)
        y = _head_rms(r[...], gho_ref[:, cols], bd).astype(BF16)
        acc = acc + jnp.dot(y, wout_ref[cols, :], preferred_element_type=F32)
    x1_ref[...] = acc
    h2_ref[...] = _row_rms(acc, gffn_ref[...]).astype(BF16)


def _mixer_out(x2d, oa, ob, oc, od, gho, wout, gffn, bd):
    t, d = x2d.shape
    full = lambda a: pl.BlockSpec(a.shape, lambda i: (0,) * a.ndim)
    tok = lambda w: pl.BlockSpec((TM, w), lambda i: (i, 0))
    return pl.pallas_call(
        _mixer_out_kernel,
        grid=(t // TM,),
        in_specs=[tok(d)] + [tok(MIX_W)] * 4 + [full(gho), full(wout), full(gffn), full(bd)],
        out_specs=[tok(d), tok(d)],
        out_shape=[jax.ShapeDtypeStruct((t, d), F32), jax.ShapeDtypeStruct((t, d), BF16)],
        compiler_params=_params("arbitrary"),
    )(x2d, oa, ob, oc, od, gho, wout, gffn, bd)


def _top16_rows(s):
    kidx = lax.broadcasted_iota(jnp.int32, s.shape, 0).astype(F32)
    rank = jnp.full(s.shape, float(PEER_TOPK), F32)
    vals = []
    for r in range(PEER_TOPK):
        m = jnp.max(s, axis=0, keepdims=True)
        first = jnp.min(jnp.where(s == m, kidx, float(PEER_N_KEYS)), axis=0, keepdims=True)
        hit = kidx == first
        rank = jnp.where(hit, float(r), rank)
        s = jnp.where(hit, -jnp.inf, s)
        vals.append(m)
    return vals, rank


def _top16_rows_unique(s, want_rank):
    rank = jnp.full(s.shape, float(PEER_TOPK), F32) if want_rank else None
    vals = []
    for r in range(PEER_TOPK):
        m = jnp.max(s, axis=0, keepdims=True)
        hit = s == m
        if want_rank:
            rank = jnp.where(hit, float(r), rank)
        s = jnp.where(hit, -jnp.inf, s)
        vals.append(m)
    hits = jnp.sum(jnp.where(s == -jnp.inf, 1.0, 0.0), axis=0, keepdims=True)
    return vals, rank, hits


def _peer_route_kernel(h2_ref, wqt_ref, keys_ref, r2_ref, e2_ref, n_ref, e1_ref, st_ref):
    tm = h2_ref.shape[0]
    qt = lax.dot_general(wqt_ref[...], h2_ref[...], _NT, preferred_element_type=F32).astype(BF16)
    for hp in range(2 * PEER_HEADS):
        rows = slice(hp * PEER_N_KEYS, (hp + 1) * PEER_N_KEYS)
        st_ref[hp] = jnp.dot(keys_ref[hp], qt[rows, :], preferred_element_type=F32)

    nlb = tm // LANES
    sub = lax.broadcasted_iota(jnp.int32, (8, LANES), 0).astype(F32)
    pos = jnp.concatenate([sub, sub + 8.0] + [sub + 16.0 * a for a in range(1, 8)] + [(sub + 8.0) * 16.0], axis=0)
    sub16 = lax.broadcasted_iota(jnp.int32, (PEER_TOPK, LANES), 0).astype(F32)

    def candidates(v1, v2):
        sv1 = jnp.zeros((PEER_TOPK, LANES), F32)
        sv2 = jnp.zeros((PEER_TOPK, LANES), F32)
        for r in range(PEER_TOPK):
            sv1 = jnp.where(sub16 == float(r), v1[r], sv1)
            sv2 = jnp.where(sub16 == float(r), v2[r], sv2)
        return jnp.concatenate(
            [v1[0] + sv2[0:8], v1[0] + sv2[8:16]] + [v1[a] + sv2[0:8] for a in range(1, 8)] + [sv1[8:16] + v2[0]],
            axis=0)

    def store(hd, lb, s1, s2, v1, v2, rk2, cnt, zsum):
        r2_ref[hd, lb] = rk2.astype(BF16)
        e2_ref[hd, lb] = jnp.exp(s2 - v2[0]).astype(BF16)
        n_ref[hd, lb] = cnt
        e1_ref[hd, lb] = jnp.exp(s1 - v1[0]) * (1.0 / zsum)

    def fast(hd, lb):
        lo = pl.multiple_of(lb * LANES, LANES)
        s1 = st_ref[2 * hd, :, pl.ds(lo, LANES)]
        s2 = st_ref[2 * hd + 1, :, pl.ds(lo, LANES)]

        v1, _, hits1 = _top16_rows_unique(s1, False)
        v2, rk2, hits2 = _top16_rows_unique(s2, True)
        cand = candidates(v1, v2)
        zsum = jnp.zeros((1, LANES), F32)
        top0 = v1[0] + v2[0]
        for r in range(PEER_TOPK):
            m = jnp.max(cand, axis=0, keepdims=True)
            cand = jnp.where(cand == m, -jnp.inf, cand)
            zsum = zsum + jnp.exp(m - top0)
        sel = jnp.where(cand == -jnp.inf, 1.0, 0.0)
        na = [jnp.sum(sel[0:16], axis=0, keepdims=True)]
        na += [jnp.sum(sel[8 + 8 * a:16 + 8 * a], axis=0, keepdims=True) for a in range(1, 8)]
        last = sel[72:80]
        cnt = jnp.zeros((PEER_N_KEYS, LANES), F32)
        for a in range(8):
            cnt = jnp.where(s1 == v1[a], na[a], cnt)
        for a in range(8, PEER_TOPK):
            cnt = jnp.where(s1 == v1[a], last[a - 8:a - 7], cnt)
        store(hd, lb, s1, s2, v1, v2, rk2, cnt, zsum)
        picked = jnp.sum(sel, axis=0, keepdims=True)
        return jnp.max(jnp.maximum(jnp.maximum(hits1, hits2), picked))

    def exact(hd, lb):
        lo = pl.multiple_of(lb * LANES, LANES)
        s1 = st_ref[2 * hd, :, pl.ds(lo, LANES)]
        s2 = st_ref[2 * hd + 1, :, pl.ds(lo, LANES)]
        v1, rk1 = _top16_rows(s1)
        v2, rk2 = _top16_rows(s2)
        cand = candidates(v1, v2)
        cnt = jnp.zeros((PEER_N_KEYS, LANES), F32)
        zsum = jnp.zeros((1, LANES), F32)
        top0 = v1[0] + v2[0]
        for r in range(PEER_TOPK):
            m = jnp.max(cand, axis=0, keepdims=True)
            pr = jnp.min(jnp.where(cand == m, pos, 4096.0), axis=0, keepdims=True)
            cand = jnp.where(pos == pr, -jnp.inf, cand)
            cnt = cnt + jnp.where(rk1 == jnp.floor(pr * (1.0 / 16.0)), 1.0, 0.0)
            zsum = zsum + jnp.exp(m - top0)
        store(hd, lb, s1, s2, v1, v2, rk2, cnt, zsum)

    def body(it, carry):
        blocks = [((it * ROUTE_BLOCKS_PER_TRIP + k) // nlb, (it * ROUTE_BLOCKS_PER_TRIP + k) % nlb)
                  for k in range(ROUTE_BLOCKS_PER_TRIP)]
        worst = [fast(hd, lb) for hd, lb in blocks]
        for (hd, lb), w in zip(blocks, worst):
            pl.when(w > PEER_TOPK + 0.5)(functools.partial(exact, hd, lb))
        return carry

    lax.fori_loop(0, PEER_HEADS * nlb // ROUTE_BLOCKS_PER_TRIP, body, 0)


def _peer_route(h2, wqt, keys):
    t, d = h2.shape
    tabs = lambda dt: jax.ShapeDtypeStruct((PEER_HEADS, t // LANES, PEER_N_KEYS, LANES), dt)
    tab_spec = pl.BlockSpec((PEER_HEADS, TM // LANES, PEER_N_KEYS, LANES), lambda i: (0, i, 0, 0))
    return pl.pallas_call(
        _peer_route_kernel,
        grid=(t // TM,),
        in_specs=[pl.BlockSpec((TM, d), lambda i: (i, 0)),
                  pl.BlockSpec(wqt.shape, lambda i: (0, 0)),
                  pl.BlockSpec(keys.shape, lambda i: (0, 0, 0))],
        out_specs=[tab_spec] * 4,
        out_shape=[tabs(BF16), tabs(BF16), tabs(F32), tabs(F32)],
        scratch_shapes=[pltpu.VMEM((2 * PEER_HEADS, PEER_N_KEYS, TM), F32)],
        compiler_params=_params("arbitrary"),
    )(h2, wqt, keys)


def _peer_ffn_kernel(h2_ref, u_ref, vt_ref, r2_ref, e2_ref, n_ref, e1_ref, x1_ref, o_ref,
                     acc_ref, at_ref, wt_ref):
    c = pl.program_id(1)
    tm = h2_ref.shape[0]
    ec = u_ref.shape[0]
    ngroup = ec // (PEER_GROUP * PEER_N_KEYS)

    @pl.when(c == 0)
    def _():
        acc_ref[...] = jnp.zeros(acc_ref.shape, F32)

    at_ref[...] = lax.dot_general(u_ref[...], h2_ref[...], _NT, preferred_element_type=F32)

    def group(gi, carry):
        grp = pl.ds(pl.multiple_of((c * ngroup + gi) * PEER_GROUP, PEER_GROUP), PEER_GROUP)
        for k in range(PEER_GROUP):
            rows = pl.ds(pl.multiple_of((gi * PEER_GROUP + k) * PEER_N_KEYS, PEER_N_KEYS), PEER_N_KEYS)
            for lb in range(tm // LANES):
                cols = slice(lb * LANES, (lb + 1) * LANES)
                gw = jnp.zeros((PEER_N_KEYS, LANES), BF16)
                for hd in range(PEER_HEADS):
                    nrow = n_ref[hd, lb, grp, :].astype(BF16)[k:k + 1, :]
                    e1row = e1_ref[hd, lb, grp, :].astype(BF16)[k:k + 1, :]
                    margin = nrow - r2_ref[hd, lb]
                    gw = gw + jnp.maximum(jnp.minimum(e1row * e2_ref[hd, lb], margin), 0.0)
                wt_ref[rows, cols] = _gelu_tanh(at_ref[rows, cols]).astype(BF16) * gw
        return carry

    lax.fori_loop(0, ngroup, group, 0)
    acc_ref[...] += jnp.dot(vt_ref[...], wt_ref[...], preferred_element_type=F32)

    @pl.when(c == pl.num_programs(1) - 1)
    def _():
        o_ref[...] = x1_ref[...] + acc_ref[...].T


def _peer_ffn(h2, u, vt, r2, e2, n, e1, x1):
    t, d = h2.shape
    ne = u.shape[0]
    tab_spec = pl.BlockSpec((PEER_HEADS, TM // LANES, PEER_N_KEYS, LANES), lambda i, c: (0, i, 0, 0))
    return pl.pallas_call(
        _peer_ffn_kernel,
        grid=(t // TM, ne // PEER_EC),
        in_specs=[pl.BlockSpec((TM, d), lambda i, c: (i, 0)),
                  pl.BlockSpec((PEER_EC, d), lambda i, c: (c, 0)),
                  pl.BlockSpec((d, PEER_EC), lambda i, c: (0, c)),
                  tab_spec, tab_spec, tab_spec, tab_spec,
                  pl.BlockSpec((TM, d), lambda i, c: (i, 0))],
        out_specs=pl.BlockSpec((TM, d), lambda i, c: (i, 0)),
        out_shape=jax.ShapeDtypeStruct((t, d), F32),
        scratch_shapes=[pltpu.VMEM((d, TM), F32), pltpu.VMEM((PEER_EC, TM), F32), pltpu.VMEM((PEER_EC, TM), BF16)],
        compiler_params=_params("arbitrary", "arbitrary"),
    )(h2, u, vt, r2, e2, n, e1, x1)


def _tile_heads(g):
    return jnp.tile(g, HEADS_PER_MIXER)[None, :]


def kernel(x, g_mix_norm, w_in, w_sgu, b_sgu, g_sgu_v, g_q_dil, g_k_dil, conv_w, g_q_sb, g_k_sb, g_head_out,
           w_out, g_ffn_norm, w_peer_q, peer_sub_keys, peer_u, peer_v):
    b, s, d = x.shape
    depth = w_in.shape[0]
    assert d == 4 * MIX_W and s % TM == 0 and s >= DIL_WIN
    assert w_in.shape[2] == N_IN_PIECES * MIX_W

    hh = np.arange(MIX_W) // HEAD_DIM
    bd = jnp.asarray(hh[:, None] == hh[None, :], BF16)
    lc = jnp.asarray(_dilated_logcount_table())
    umat = jnp.asarray(_suffix_sum_matrix(), BF16)

    x2d = x.reshape(b * s, d)
    for l in range(depth):
        wsgu_cat = w_sgu[l].transpose(1, 0, 2).reshape(SGU_CHUNK, HEADS_PER_MIXER * SGU_CHUNK)
        bsgu_tile = jnp.repeat(b_sgu[l].T, HEAD_DIM, axis=1)
        oa, oc, dq, dk, dv, bq, bk, bv = _mixer_in(
            x2d, g_mix_norm[l][None, :], w_in[l].astype(BF16), wsgu_cat, bsgu_tile, g_sgu_v[l][None, :],
            _tile_heads(g_q_dil[l]), _tile_heads(g_k_dil[l]), conv_w[l], _tile_heads(g_q_sb[l]),
            _tile_heads(g_k_sb[l]), bd, seq=s)
        r3 = lambda a: a.reshape(b, s, MIX_W)
        ob = _dilated(r3(dq), r3(dk), r3(dv), lc).reshape(b * s, MIX_W)
        od = _stickbreak(r3(bq), r3(bk), r3(bv), umat).reshape(b * s, MIX_W)
        x1, h2 = _mixer_out(x2d, oa, ob, oc, od, g_head_out[l][None, :], w_out[l].astype(BF16),
                            g_ffn_norm[l][None, :], bd)
        wqt = w_peer_q[l].T.astype(BF16)
        keys = peer_sub_keys[l].reshape(2 * PEER_HEADS, PEER_N_KEYS, -1).astype(BF16)
        r2, e2, n, e1 = _peer_route(h2, wqt, keys)
        x2d = _peer_ffn(h2, peer_u[l].astype(BF16), peer_v[l].T.astype(BF16), r2, e2, n, e1, x1)
    return x2d.reshape(b, s, d)
```
